```python
import jax
import jax.numpy as jnp
from jax import lax
import numpy as np

D_MODEL = 2048
BATCH = 8
SEQ = 2048
DEPTH = 1
DEC_BATCH = 32
DEC_SEQ = 1
PAST_LEN = 16384
PAGE_SIZE = 128

N_MEM = 256
SBA_HEAD_DIM = 128
SBA_HEADS = (D_MODEL // 2) // SBA_HEAD_DIM
SBA_BIAS_INIT = -6.0
GLA_HEADS = 4
GLA_V_DIM = (D_MODEL // 4) // GLA_HEADS
GLA_K_DIM = GLA_V_DIM // 2
GLA_GATE_RANK = 16
GLA_TAU = 16.0
GLA_CHUNK = 32
MEM_HEADS = 4
MEM_HEAD_DIM = (D_MODEL // 4) // MEM_HEADS
Q_BLOCK = 128
D_FF = 4 * D_MODEL
LN_EPS = 1e-5
NORM_EPS = 1e-6
DEEPNORM_ALPHA = (2.0 * DEPTH) ** 0.25
DEEPNORM_BETA = (8.0 * DEPTH) ** -0.25

SBA_W = SBA_HEADS * SBA_HEAD_DIM
GLA_KW = GLA_HEADS * GLA_K_DIM
GLA_VW = GLA_HEADS * GLA_V_DIM
MEM_W = MEM_HEADS * MEM_HEAD_DIM
MIX_W = SBA_W + GLA_VW + MEM_W
IN_SIZES = (SBA_W, SBA_W, SBA_W, GLA_KW, GLA_KW, GLA_VW, GLA_VW, GLA_GATE_RANK, MEM_W)
IN_W = sum(IN_SIZES)
SPLIT_IDX = tuple(sum(IN_SIZES[: i + 1]) for i in range(len(IN_SIZES) - 1))

kernel_name = 'stick_breaking_gla_memory_hybrid_step'


def layer_norm(x, g, b):
    xf = x.astype(jnp.float32)
    mu = jnp.mean(xf, axis=-1, keepdims=True)
    xc = xf - mu
    var = jnp.mean(xc * xc, axis=-1, keepdims=True)
    return (xc * lax.rsqrt(var + LN_EPS) * g.astype(jnp.float32) + b.astype(jnp.float32)).astype(x.dtype)


def split_heads(t, n_heads):
    return t.reshape(t.shape[:-1] + (n_heads, t.shape[-1] // n_heads))


def mixer_inputs(x, w_in, w_gate_up, b_gate):
    h = jnp.einsum('btd,de->bte', x, w_in)
    sq, sk, sv, gq, gk, gv, gr, glow, mq = jnp.split(h, SPLIT_IDX, axis=-1)
    sq = split_heads(sq, SBA_HEADS)
    sk = split_heads(sk, SBA_HEADS)
    sv = split_heads(sv, SBA_HEADS)
    gq = split_heads(gq, GLA_HEADS).astype(jnp.float32) * (GLA_K_DIM ** -0.5)
    gk = split_heads(gk, GLA_HEADS).astype(jnp.float32)
    gv = split_heads(gv, GLA_HEADS).astype(jnp.float32)
    glog = jax.nn.log_sigmoid((jnp.einsum('btr,rk->btk', glow, w_gate_up) + b_gate).astype(jnp.float32)) / GLA_TAU
    gg = split_heads(glog, GLA_HEADS)
    mq = split_heads(mq, MEM_HEADS)
    return sq, sk, sv, gq, gk, gv, gr, gg, mq


def sba_attend(q, k, v, q_pos, k_pos, bias):
    z = jnp.einsum('bqhd,bkhd->bhqk', q.astype(jnp.float32), k.astype(jnp.float32)) * (SBA_HEAD_DIM ** -0.5)
    z = z + bias.astype(jnp.float32)[None, :, None, None]
    mask = k_pos[None, :] < q_pos[:, None]
    log_beta = jax.nn.log_sigmoid(z)
    log_keep = jnp.where(mask, jax.nn.log_sigmoid(-z), 0.0)
    rest = lax.cumsum(log_keep, axis=3, reverse=True) - log_keep
    weights = jnp.where(mask, jnp.exp(log_beta + rest), 0.0)
    return jnp.einsum('bhqk,bkhd->bqhd', weights, v.astype(jnp.float32))


def sba_prompt(q, k, v, bias):
    b, t, h, d = q.shape
    nb = t // Q_BLOCK
    qb = q.reshape(b, nb, Q_BLOCK, h, d).transpose(1, 0, 2, 3, 4)
    pos = jnp.arange(t, dtype=jnp.int32)
    pb = pos.reshape(nb, Q_BLOCK)
    out = lax.map(lambda a: sba_attend(a[0], k, v, a[1], pos, bias), (qb, pb))
    return out.transpose(1, 0, 2, 3, 4).reshape(b, t, h * d)


def gla_chunked(q, k, v, g, s0):
    b, t, h, dk = q.shape
    dv = v.shape[-1]
    nc = t // GLA_CHUNK

    def chunks(a):
        return a.reshape(b, nc, GLA_CHUNK, h, a.shape[-1]).transpose(1, 0, 3, 2, 4)

    causal = jnp.tril(jnp.ones((GLA_CHUNK, GLA_CHUNK), dtype=bool))

    def step(s, inp):
        qc, kc, vc, gc = inp
        gcum = jnp.cumsum(gc, axis=2)
        inter = jnp.einsum('bhid,bhde->bhie', qc * jnp.exp(gcum), s)
        diff = gcum[:, :, :, None, :] - gcum[:, :, None, :, :]
        decay = jnp.exp(jnp.where(causal[:, :, None], diff, -jnp.inf))
        scores = jnp.einsum('bhid,bhjd,bhijd->bhij', qc, kc, decay)
        intra = jnp.einsum('bhij,bhje->bhie', scores, vc)
        g_last = gcum[:, :, -1:, :]
        s_new = jnp.exp(g_last[:, :, 0, :])[..., None] * s + jnp.einsum('bhjd,bhje->bhde', kc * jnp.exp(g_last - gcum), vc)
        return s_new, inter + intra

    s_fin, o = lax.scan(step, s0, (chunks(q), chunks(k), chunks(v), chunks(g)))
    o = o.transpose(1, 0, 3, 2, 4).reshape(b, t, h, dv)
    return o, s_fin


def gla_recurrent(q, k, v, g, s0):
    def step(s, inp):
        qt, kt, vt, gt = inp
        s = jnp.exp(gt)[..., None] * s + kt[..., :, None] * vt[..., None, :]
        return s, jnp.einsum('bhd,bhde->bhe', qt, s)

    tm = lambda a: jnp.moveaxis(a, 1, 0)
    s_fin, o = lax.scan(step, s0, (tm(q), tm(k), tm(v), tm(g)))
    return jnp.moveaxis(o, 0, 1), s_fin


def mem_attend(q, mk, mv):
    s = jnp.einsum('bthd,bmhd->bhtm', q.astype(jnp.float32), mk.astype(jnp.float32)) * (MEM_HEAD_DIM ** -0.5)
    p = jax.nn.softmax(s, axis=-1)
    o = jnp.einsum('bhtm,bmhd->bthd', p, mv.astype(jnp.float32))
    return o.reshape(o.shape[0], o.shape[1], MEM_W)


def layer_tail(x, o_sba, o_gla, r_gla, o_mem, gla_norm_g, w_o, ln1_g, ln1_b, w_up, w_down, ln2_g, ln2_b):
    b, t = x.shape[0], x.shape[1]
    og = o_gla * lax.rsqrt(jnp.mean(o_gla * o_gla, axis=-1, keepdims=True) + NORM_EPS)
    og = og.reshape(b, t, GLA_VW) * gla_norm_g.astype(jnp.float32) * jax.nn.silu(r_gla.astype(jnp.float32))
    mixed = jnp.concatenate([o_sba, og, o_mem], axis=-1).astype(x.dtype)
    x1 = layer_norm(DEEPNORM_ALPHA * x + jnp.einsum('bte,ed->btd', mixed, w_o), ln1_g, ln1_b)
    u = jnp.square(jax.nn.relu(jnp.einsum('btd,df->btf', x1, w_up)))
    return layer_norm(DEEPNORM_ALPHA * x1 + jnp.einsum('btf,fd->btd', u, w_down), ln2_g, ln2_b)


def setup_inputs(seed: int = 0) -> dict:
    key = jax.random.key(seed)
    keys = jax.random.split(key, 24)
    n_pages = PAST_LEN // PAGE_SIZE
    n_phys = (DEC_BATCH * n_pages * 5) // 4

    def nrm(k, shape, scale=1.0):
        return jax.random.normal(k, shape, jnp.float32) * scale

    perm = jax.random.permutation(keys[0], n_phys)
    page_table = perm[: DEC_BATCH * n_pages].reshape(DEC_BATCH, n_pages).astype(jnp.int32)
    return {
        'x_prompt': nrm(keys[1], (BATCH, SEQ, D_MODEL)),
        'x_sample': nrm(keys[2], (DEC_BATCH, DEC_SEQ, D_MODEL)),
        'cache_sba_k': nrm(keys[3], (DEPTH, n_phys, PAGE_SIZE, SBA_HEADS, SBA_HEAD_DIM)),
        'cache_sba_v': nrm(keys[4], (DEPTH, n_phys, PAGE_SIZE, SBA_HEADS, SBA_HEAD_DIM)),
        'state_gla': nrm(keys[5], (DEPTH, DEC_BATCH, GLA_HEADS, GLA_K_DIM, GLA_V_DIM)),
        'cache_mem_k': nrm(keys[6], (DEPTH, DEC_BATCH, N_MEM, MEM_HEADS, MEM_HEAD_DIM)),
        'cache_mem_v': nrm(keys[7], (DEPTH, DEC_BATCH, N_MEM, MEM_HEADS, MEM_HEAD_DIM)),
        'page_table': page_table,
        'mem_prompt': nrm(keys[8], (BATCH, N_MEM, D_MODEL)),
        'w_in': nrm(keys[9], (DEPTH, D_MODEL, IN_W), D_MODEL ** -0.5),
        'sba_bias': SBA_BIAS_INIT + nrm(keys[22], (DEPTH, SBA_HEADS), 0.1),
        'w_gate_up': nrm(keys[10], (DEPTH, GLA_GATE_RANK, GLA_KW), GLA_GATE_RANK ** -0.5),
        'b_gate': nrm(keys[11], (DEPTH, GLA_KW), 0.1),
        'gla_norm_g': 1.0 + nrm(keys[12], (DEPTH, GLA_VW), 0.02),
        'w_mem_k': nrm(keys[13], (DEPTH, D_MODEL, MEM_W), D_MODEL ** -0.5),
        'w_mem_v': nrm(keys[14], (DEPTH, D_MODEL, MEM_W), D_MODEL ** -0.5),
        'w_o': nrm(keys[15], (DEPTH, MIX_W, D_MODEL), (MIX_W ** -0.5) * DEEPNORM_BETA),
        'ln1_g': 1.0 + nrm(keys[16], (DEPTH, D_MODEL), 0.02),
        'ln1_b': nrm(keys[17], (DEPTH, D_MODEL), 0.02),
        'w_up': nrm(keys[18], (DEPTH, D_MODEL, D_FF), D_MODEL ** -0.5),
        'w_down': nrm(keys[19], (DEPTH, D_FF, D_MODEL), (D_FF ** -0.5) * DEEPNORM_BETA),
        'ln2_g': 1.0 + nrm(keys[20], (DEPTH, D_MODEL), 0.02),
        'ln2_b': nrm(keys[21], (DEPTH, D_MODEL), 0.02),
    }


def reference(x_prompt, x_sample, cache_sba_k, cache_sba_v, state_gla, cache_mem_k, cache_mem_v, page_table,
              mem_prompt, w_in, sba_bias, w_gate_up, b_gate, gla_norm_g, w_mem_k, w_mem_v, w_o, ln1_g, ln1_b,
              w_up, w_down, ln2_g, ln2_b):
    n_pages = PAST_LEN // PAGE_SIZE
    bp = x_prompt.shape[0]
    bs, ts = x_sample.shape[0], x_sample.shape[1]
    yp, ys = x_prompt, x_sample
    kp_l, vp_l, sp_l, mkp_l, mvp_l, ks_l, vs_l, ss_l = [], [], [], [], [], [], [], []
    for l in range(DEPTH):
        tail = (gla_norm_g[l], w_o[l], ln1_g[l], ln1_b[l], w_up[l], w_down[l], ln2_g[l], ln2_b[l])

        sq, sk, sv, gq, gk, gv, gr, gg, mq = mixer_inputs(yp, w_in[l], w_gate_up[l], b_gate[l])
        o_sba = sba_prompt(sq, sk, sv, sba_bias[l])
        s0 = jnp.zeros((bp, GLA_HEADS, GLA_K_DIM, GLA_V_DIM), jnp.float32)
        o_gla, s_p = gla_chunked(gq, gk, gv, gg, s0)
        mk = split_heads(jnp.einsum('bmd,de->bme', mem_prompt, w_mem_k[l]), MEM_HEADS)
        mv = split_heads(jnp.einsum('bmd,de->bme', mem_prompt, w_mem_v[l]), MEM_HEADS)
        o_mem = mem_attend(mq, mk, mv)
        yp_next = layer_tail(yp, o_sba, o_gla, gr, o_mem, *tail)
        kp_l.append(sk)
        vp_l.append(sv)
        sp_l.append(s_p)
        mkp_l.append(mk)
        mvp_l.append(mv)

        sq2, sk2, sv2, gq2, gk2, gv2, gr2, gg2, mq2 = mixer_inputs(ys, w_in[l], w_gate_up[l], b_gate[l])
        k_past = cache_sba_k[l][page_table].reshape(bs, n_pages * PAGE_SIZE, SBA_HEADS, SBA_HEAD_DIM)
        v_past = cache_sba_v[l][page_table].reshape(bs, n_pages * PAGE_SIZE, SBA_HEADS, SBA_HEAD_DIM)
        k_all = jnp.concatenate([k_past, sk2.astype(k_past.dtype)], axis=1)
        v_all = jnp.concatenate([v_past, sv2.astype(v_past.dtype)], axis=1)
        q_pos = PAST_LEN + jnp.arange(ts, dtype=jnp.int32)
        k_pos = jnp.arange(n_pages * PAGE_SIZE + ts, dtype=jnp.int32)
        o_sba2 = sba_attend(sq2, k_all, v_all, q_pos, k_pos, sba_bias[l]).reshape(bs, ts, SBA_W)
        o_gla2, s_s = gla_recurrent(gq2, gk2, gv2, gg2, state_gla[l].astype(jnp.float32))
        o_mem2 = mem_attend(mq2, cache_mem_k[l], cache_mem_v[l])
        ys_next = layer_tail(ys, o_sba2, o_gla2, gr2, o_mem2, *tail)
        ks_l.append(sk2)
        vs_l.append(sv2)
        ss_l.append(s_s)

        yp, ys = yp_next, ys_next
    return (yp, ys, jnp.stack(kp_l), jnp.stack(vp_l), jnp.stack(sp_l), jnp.stack(mkp_l), jnp.stack(mvp_l),
            jnp.stack(ks_l), jnp.stack(vs_l), jnp.stack(ss_l))
```

```python
import functools
import math

import jax
import jax.numpy as jnp
from jax import lax
from jax.experimental import pallas as pl
from jax.experimental.pallas import tpu as pltpu

F32 = jnp.float32
BF16 = jnp.bfloat16

LANES = 128
LN_EPS = 1e-5
NORM_EPS = 1e-6
GLA_TAU = 16.0
VMEM_LIMIT_BYTES = 56 * 1024 * 1024


def _params(sem):
    return pltpu.CompilerParams(dimension_semantics=sem, vmem_limit_bytes=VMEM_LIMIT_BYTES)


def _softplus(z):
    return jnp.maximum(z, 0.0) + jnp.log1p(jnp.exp(-jnp.abs(z)))


def _layer_norm(v, g, b):
    mu = jnp.mean(v, axis=-1, keepdims=True)
    vc = v - mu
    var = jnp.mean(vc * vc, axis=-1, keepdims=True)
    return vc * lax.rsqrt(var + LN_EPS) * g + b


def _pack_w_in(w_in, w_gate_up, b_gate, dims):
    sba_w, gkw, gvw, rank, mem_w, gh, gk = dims
    o = [0]
    for s in (sba_w, sba_w, sba_w, gkw, gkw, gvw, gvw, rank, mem_w):
        o.append(o[-1] + s)
    d = w_in.shape[0]

    def pad_heads(w):
        w = w.reshape(d, gh, gk)
        return jnp.pad(w, ((0, 0), (0, 0), (0, LANES - gk))).reshape(d, gh * LANES)

    cols = [w_in[:, o[0]:o[3]], pad_heads(w_in[:, o[3]:o[4]]), pad_heads(w_in[:, o[4]:o[5]]),
            w_in[:, o[5]:o[7]], w_in[:, o[8]:o[9]],
            jnp.pad(w_in[:, o[7]:o[8]], ((0, 0), (0, LANES - rank)))]
    w_packed = jnp.concatenate(cols, axis=1).astype(BF16)
    wg = jnp.pad(w_gate_up.reshape(rank, gh, gk), ((0, LANES - rank), (0, 0), (0, LANES - gk)))
    wg = wg.reshape(LANES, gh * LANES).astype(BF16)
    bg = jnp.pad(b_gate.reshape(1, gh, gk), ((0, 0), (0, 0), (0, LANES - gk))).reshape(1, gh * LANES)
    return w_packed, wg, bg.astype(F32)


def _proj_kernel(x_ref, w_ref, wg_ref, bg_ref,
                 q_ref, kf_ref, vf_ref, kb_ref, vb_ref, gq_ref, gk_ref, gv_ref, gr_ref, gg_ref, mq_ref,
                 *, sba_w, gp_w, gvw, mem_w, sba_scale, gla_scale, mem_scale):
    xb = x_ref[...].astype(BF16)

    def mm(lo, width):
        return jnp.dot(xb, w_ref[:, lo:lo + width], preferred_element_type=F32)

    c = 0
    q_ref[...] = (mm(c, sba_w) * sba_scale).astype(BF16); c += sba_w
    k = mm(c, sba_w); c += sba_w
    kf_ref[...] = k
    kb_ref[...] = k.astype(BF16)
    v = mm(c, sba_w); c += sba_w
    vf_ref[...] = v
    vb_ref[...] = v.astype(BF16)
    gq_ref[...] = mm(c, gp_w) * gla_scale; c += gp_w
    gk_ref[...] = mm(c, gp_w); c += gp_w
    gv_ref[...] = mm(c, gvw); c += gvw
    gr_ref[...] = mm(c, gvw); c += gvw
    mq_ref[...] = (mm(c, mem_w) * mem_scale).astype(BF16); c += mem_w
    glow = mm(c, LANES)
    pre = jnp.dot(glow.astype(BF16), wg_ref[...], preferred_element_type=F32) + bg_ref[...]
    gg_ref[...] = -_softplus(-pre) * (1.0 / GLA_TAU)


def _project(x2d, w_packed, wg, bg, *, tm, sba_w, gp_w, gvw, mem_w, sba_d, gla_k, mem_d):
    m, d = x2d.shape
    npk = w_packed.shape[1]
    kern = functools.partial(_proj_kernel, sba_w=sba_w, gp_w=gp_w, gvw=gvw, mem_w=mem_w,
                             sba_scale=sba_d ** -0.5, gla_scale=gla_k ** -0.5, mem_scale=mem_d ** -0.5)
    row = lambda w: pl.BlockSpec((tm, w), lambda i: (i, 0))
    const = lambda shape: pl.BlockSpec(shape, lambda i: (0, 0), pipeline_mode=pl.Buffered(1))
    out_shapes = [(sba_w, BF16), (sba_w, F32), (sba_w, F32), (sba_w, BF16), (sba_w, BF16),
                  (gp_w, F32), (gp_w, F32), (gvw, F32), (gvw, F32), (gp_w, F32), (mem_w, BF16)]
    return pl.pallas_call(
        kern,
        grid=(m // tm,),
        in_specs=[row(d), const((d, npk)), const(wg.shape), const(bg.shape)],
        out_specs=[row(w) for w, _ in out_shapes],
        out_shape=[jax.ShapeDtypeStruct((m, w), dt) for w, dt in out_shapes],
        compiler_params=_params(("parallel",)),
        name="in_proj",
    )(x2d, w_packed, wg, bg)


def _sba_tile(q, k, v, bias, run, acc, u_ref, diag):
    z = lax.dot_general(q, k, (((1,), (1,)), ((), ())), preferred_element_type=F32) + bias
    sp = _softplus(z)
    log_keep = -sp
    log_beta = z - sp
    if diag:
        tq, tk = z.shape
        keep = lax.broadcasted_iota(jnp.int32, (tq, tk), 1) < lax.broadcasted_iota(jnp.int32, (tq, tk), 0)
        log_keep = jnp.where(keep, log_keep, 0.0)
        log_beta = jnp.where(keep, log_beta, -1e30)
    rest = jnp.dot(log_keep.astype(BF16), u_ref[...], preferred_element_type=F32)
    w = jnp.exp(log_beta + rest + run)
    acc = acc + jnp.dot(w.astype(BF16), v, preferred_element_type=F32)
    run = run + jnp.sum(log_keep, axis=-1, keepdims=True)
    return run, acc


def _sba_prompt_kernel(bias_ref, q_ref, k_ref, v_ref, u_ref, o_ref, *, tq, hd, heads_per_step):
    i = pl.program_id(2)
    hg = pl.program_id(1)

    for hh in range(heads_per_step):
        lanes = slice(hh * hd, (hh + 1) * hd)
        bias = bias_ref[hg * heads_per_step + hh]
        q = q_ref[:, lanes]
        start = pl.multiple_of(i * tq, tq)
        run0 = jnp.zeros((tq, 1), F32)
        acc0 = jnp.zeros((tq, hd), F32)
        run, acc = _sba_tile(q, k_ref[pl.ds(start, tq), lanes], v_ref[pl.ds(start, tq), lanes],
                             bias, run0, acc0, u_ref, True)

        def body(jj, carry):
            run, acc = carry
            s = pl.multiple_of((i - 1 - jj) * tq, tq)
            return _sba_tile(q, k_ref[pl.ds(s, tq), lanes], v_ref[pl.ds(s, tq), lanes],
                             bias, run, acc, u_ref, False)

        run, acc = lax.fori_loop(0, i, body, (run, acc))
        o_ref[:, lanes] = acc.astype(o_ref.dtype)


def _sba_prompt(q, k, v, bias, *, batch, seq, heads, hd, tq=256, heads_per_step=2):
    m = q.shape[0]
    nq = seq // tq
    w = heads_per_step * hd
    u = (lax.broadcasted_iota(jnp.int32, (tq, tq), 0) > lax.broadcasted_iota(jnp.int32, (tq, tq), 1)).astype(BF16)
    kern = functools.partial(_sba_prompt_kernel, tq=tq, hd=hd, heads_per_step=heads_per_step)
    return pl.pallas_call(
        kern,
        grid=(batch, heads // heads_per_step, nq),
        in_specs=[pl.BlockSpec(memory_space=pltpu.SMEM),
                  pl.BlockSpec((tq, w), lambda b, h, i: (b * nq + i, h)),
                  pl.BlockSpec((seq, w), lambda b, h, i: (b, h)),
                  pl.BlockSpec((seq, w), lambda b, h, i: (b, h)),
                  pl.BlockSpec((tq, tq), lambda b, h, i: (0, 0))],
        out_specs=pl.BlockSpec((tq, w), lambda b, h, i: (b * nq + i, h)),
        out_shape=jax.ShapeDtypeStruct((m, heads * hd), BF16),
        compiler_params=_params(("parallel", "parallel", "arbitrary")),
        name="sba_prompt",
    )(bias, q, k, v, u)


def _gla_gate_out(o, r, g):
    og = o * lax.rsqrt(jnp.mean(o * o, axis=-1, keepdims=True) + NORM_EPS)
    return og * g * (r * jax.nn.sigmoid(r))


def _gla_prompt_kernel(q_ref, k_ref, v_ref, g_ref, r_ref, ng_ref, tri_ref, o_ref, s_ref, *, chunk, n_chunks, dk):
    tri = tri_ref[...]
    causal = tri > 0.5
    half = chunk // 2
    st = jnp.zeros((LANES, LANES), F32)
    for c in range(n_chunks):
        rows = slice(c * chunk, (c + 1) * chunk)
        q = q_ref[rows, :]
        k = k_ref[rows, :]
        v = v_ref[rows, :]
        gc = jnp.dot(tri, g_ref[rows, :], preferred_element_type=F32, precision=lax.Precision.HIGHEST)
        g_last = gc[chunk - 1:chunk, :]
        g_mid = gc[half - 1:half, :]
        inter = lax.dot_general((q * jnp.exp(gc)).astype(BF16), st.astype(BF16),
                                (((1,), (1,)), ((), ())), preferred_element_type=F32)
        qi = (q * jnp.exp(gc - g_mid)).astype(BF16)
        ki = (k * jnp.exp(g_mid - gc)).astype(BF16)
        scores = lax.dot_general(qi, ki, (((1,), (1,)), ((), ())), preferred_element_type=F32)
        scores = jnp.where(causal, scores, 0.0)
        intra = jnp.dot(scores.astype(BF16), v.astype(BF16), preferred_element_type=F32)
        o = inter + intra
        o_ref[rows, :] = _gla_gate_out(o, r_ref[rows, :], ng_ref[...]).astype(o_ref.dtype)
        kd = (k * jnp.exp(g_last - gc)).astype(BF16)
        st = st * jnp.exp(g_last) + lax.dot_general(v.astype(BF16), kd, (((0,), (0,)), ((), ())),
                                                    preferred_element_type=F32)
    s_ref[...] = st.T[:dk, :]


def _gla_prompt(gq, gk, gv, gg, gr, norm_g, *, batch, seq, heads, dk, chunk=128):
    m = gq.shape[0]
    n_chunks = seq // chunk
    tri = (lax.broadcasted_iota(jnp.int32, (chunk, chunk), 0)
           >= lax.broadcasted_iota(jnp.int32, (chunk, chunk), 1)).astype(F32)
    kern = functools.partial(_gla_prompt_kernel, chunk=chunk, n_chunks=n_chunks, dk=dk)
    blk = pl.BlockSpec((seq, LANES), lambda b, h: (b, h))
    return pl.pallas_call(
        kern,
        grid=(batch, heads),
        in_specs=[blk, blk, blk, blk, blk,
                  pl.BlockSpec((1, LANES), lambda b, h: (0, h)),
                  pl.BlockSpec((chunk, chunk), lambda b, h: (0, 0))],
        out_specs=[blk, pl.BlockSpec((None, None, dk, LANES), lambda b, h: (b, h, 0, 0))],
        out_shape=[jax.ShapeDtypeStruct((m, heads * LANES), BF16),
                   jax.ShapeDtypeStruct((batch, heads, dk, LANES), F32)],
        compiler_params=_params(("parallel", "parallel")),
        name="gla_prompt",
    )(gq, gk, gv, gg, gr, norm_g, tri)


def _mem_kv_kernel(x_ref, wk_ref, wv_ref, kf_ref, vf_ref, kb_ref, vb_ref):
    xb = x_ref[...].astype(BF16)
    k = jnp.dot(xb, wk_ref[...], preferred_element_type=F32)
    v = jnp.dot(xb, wv_ref[...], preferred_element_type=F32)
    kf_ref[...] = k
    vf_ref[...] = v
    kb_ref[...] = k.astype(BF16)
    vb_ref[...] = v.astype(BF16)


def _mem_kv(mem2d, wk, wv, *, tm=512):
    m, d = mem2d.shape
    w = wk.shape[1]
    row = lambda width: pl.BlockSpec((tm, width), lambda i: (i, 0))
    const = pl.BlockSpec((d, w), lambda i: (0, 0))
    return pl.pallas_call(
        _mem_kv_kernel,
        grid=(m // tm,),
        in_specs=[row(d), const, const],
        out_specs=[row(w)] * 4,
        out_shape=[jax.ShapeDtypeStruct((m, w), F32)] * 2 + [jax.ShapeDtypeStruct((m, w), BF16)] * 2,
        compiler_params=_params(("parallel",)),
        name="mem_kv",
    )(mem2d, wk, wv)


def _softmax_rows(s):
    s = s - jnp.max(s, axis=-1, keepdims=True)
    p = jnp.exp(s)
    return p / jnp.sum(p, axis=-1, keepdims=True)


def _wo_ln_kernel(*refs, alpha, sba_w, gvw, mem_heads, hd, fuse_mem):
    if fuse_mem:
        x_ref, a_ref, g_ref, mq_ref, mk_ref, mv_ref, wo_ref, lg_ref, lb_ref, o_ref = refs
    else:
        x_ref, a_ref, g_ref, om_ref, wo_ref, lg_ref, lb_ref, o_ref = refs
    acc = alpha * x_ref[...]
    acc = acc + jnp.dot(a_ref[...], wo_ref[0:sba_w, :], preferred_element_type=F32)
    acc = acc + jnp.dot(g_ref[...], wo_ref[sba_w:sba_w + gvw, :], preferred_element_type=F32)
    base = sba_w + gvw
    if fuse_mem:
        for h in range(mem_heads):
            lanes = slice(h * hd, (h + 1) * hd)
            s = lax.dot_general(mq_ref[:, lanes], mk_ref[:, lanes], (((1,), (1,)), ((), ())),
                                preferred_element_type=F32)
            p = _softmax_rows(s)
            om = jnp.dot(p.astype(BF16), mv_ref[:, lanes], preferred_element_type=F32)
            acc = acc + jnp.dot(om.astype(BF16), wo_ref[base + h * hd:base + (h + 1) * hd, :],
                                preferred_element_type=F32)
    else:
        acc = acc + jnp.dot(om_ref[...], wo_ref[base:, :], preferred_element_type=F32)
    o_ref[...] = _layer_norm(acc, lg_ref[...], lb_ref[...])


def _wo_ln(x2d, o_sba, og, mem_args, wo, ln_g, ln_b, *, tm, alpha, mem_heads, hd, rows_per_batch=None):
    m, d = x2d.shape
    sba_w, gvw = o_sba.shape[1], og.shape[1]
    fuse_mem = rows_per_batch is not None
    row = lambda w: pl.BlockSpec((tm, w), lambda i: (i, 0))
    const = lambda shape: pl.BlockSpec(shape, lambda i: (0, 0), pipeline_mode=pl.Buffered(1))
    if fuse_mem:
        mq, mk, mv = mem_args
        n_mem = mk.shape[0] // (m // rows_per_batch)
        tiles_per_batch = rows_per_batch // tm
        per_batch = pl.BlockSpec((n_mem, mk.shape[1]), lambda i: (i // tiles_per_batch, 0))
        mem_specs = [row(mq.shape[1]), per_batch, per_batch]
    else:
        mem_specs = [row(mem_args[0].shape[1])]
    kern = functools.partial(_wo_ln_kernel, alpha=alpha, sba_w=sba_w, gvw=gvw, mem_heads=mem_heads, hd=hd,
                             fuse_mem=fuse_mem)
    return pl.pallas_call(
        kern,
        grid=(m // tm,),
        in_specs=[row(d), row(sba_w), row(gvw)] + mem_specs + [const(wo.shape), const(ln_g.shape), const(ln_b.shape)],
        out_specs=row(d),
        out_shape=jax.ShapeDtypeStruct((m, d), F32),
        compiler_params=_params(("parallel",)),
        name="wo_ln1",
    )(x2d, o_sba, og, *mem_args, wo, ln_g, ln_b)


def _mlp_kernel(x_ref, wu_ref, wd_ref, lg_ref, lb_ref, o_ref, xb_ref, acc_ref, *, alpha):
    f = pl.program_id(1)

    @pl.when(f == 0)
    def _():
        xb_ref[...] = x_ref[...].astype(BF16)
        acc_ref[...] = jnp.zeros_like(acc_ref)

    h = jnp.dot(xb_ref[...], wu_ref[...], preferred_element_type=F32)
    u = jnp.square(jnp.maximum(h, 0.0))
    acc_ref[...] += jnp.dot(u.astype(BF16), wd_ref[...], preferred_element_type=F32)

    @pl.when(f == pl.num_programs(1) - 1)
    def _():
        o_ref[...] = _layer_norm(alpha * x_ref[...] + acc_ref[...], lg_ref[...], lb_ref[...])


def _mlp_ln(x1, wu, wd, ln_g, ln_b, *, tm, tf, alpha):
    m, d = x1.shape
    dff = wu.shape[1]
    kern = functools.partial(_mlp_kernel, alpha=alpha)
    return pl.pallas_call(
        kern,
        grid=(m // tm, dff // tf),
        in_specs=[pl.BlockSpec((tm, d), lambda i, f: (i, 0)),
                  pl.BlockSpec((d, tf), lambda i, f: (0, f)),
                  pl.BlockSpec((tf, d), lambda i, f: (f, 0)),
                  pl.BlockSpec((1, d), lambda i, f: (0, 0)),
                  pl.BlockSpec((1, d), lambda i, f: (0, 0))],
        out_specs=pl.BlockSpec((tm, d), lambda i, f: (i, 0)),
        out_shape=jax.ShapeDtypeStruct((m, d), F32),
        scratch_shapes=[pltpu.VMEM((tm, d), BF16), pltpu.VMEM((tm, d), F32)],
        compiler_params=_params(("parallel", "arbitrary")),
        name="mlp_ln2",
    )(x1, wu, wd, ln_g, ln_b)


def _sba_decode_kernel(pt_ref, bias_ref, q_ref, *refs, pages_per_step, heads, hd):
    k_refs = refs[:pages_per_step]
    v_refs = refs[pages_per_step:2 * pages_per_step]
    u_ref = refs[2 * pages_per_step]
    o_ref = refs[2 * pages_per_step + 1]
    acc_ref, run_ref = refs[2 * pages_per_step + 2:]
    c = pl.program_id(1)
    w = heads * hd

    @pl.when(c == 0)
    def _():
        acc_ref[...] = jnp.zeros_like(acc_ref)
        run_ref[...] = jnp.zeros_like(run_ref)

    head_of_lane = lax.broadcasted_iota(jnp.int32, (heads, w), 1) // hd
    own = head_of_lane == lax.broadcasted_iota(jnp.int32, (heads, w), 0)
    q_bd = jnp.where(own, jnp.broadcast_to(q_ref[...].astype(F32), (heads, w)), 0.0)
    acc = acc_ref[...]
    run = run_ref[...]
    for j in range(pages_per_step):
        z = lax.dot_general(q_bd, k_refs[j][...], (((1,), (1,)), ((), ())), preferred_element_type=F32)
        z = z + bias_ref[...]
        sp = _softplus(z)
        log_keep = -sp
        log_beta = z - sp
        incl = jnp.dot(log_keep, u_ref[...], preferred_element_type=F32, precision=lax.Precision.HIGHEST)
        wgt = jnp.exp(log_beta + (incl - log_keep) + run)
        acc = acc + jnp.dot(wgt, v_refs[j][...], preferred_element_type=F32)
        run = run + jnp.broadcast_to(incl[:, 0:1], run.shape)
    acc_ref[...] = acc
    run_ref[...] = run

    @pl.when(c == pl.num_programs(1) - 1)
    def _():
        o_ref[...] = jnp.sum(jnp.where(own, acc, 0.0), axis=0, keepdims=True)


def _sba_decode(q, cache_k, cache_v, page_table, bias, *, pages_per_step=8):
    n_seq, n_pages = page_table.shape
    n_phys, page, heads, hd = cache_k.shape
    w = heads * hd
    n_steps = n_pages // pages_per_step
    ck = cache_k.reshape(n_phys, page, w)
    cv = cache_v.reshape(n_phys, page, w)
    u = (lax.broadcasted_iota(jnp.int32, (page, page), 0) >= lax.broadcasted_iota(jnp.int32, (page, page), 1)).astype(F32)
    bias_col = jnp.broadcast_to(bias.reshape(heads, 1), (heads, page)).astype(F32)

    def page_spec(j):
        def index(b, c, pt):
            logical = n_pages - 1 - (c * pages_per_step + j)
            return (pt[b * n_pages + logical], 0, 0)
        return pl.BlockSpec((None, page, w), index)

    kern = functools.partial(_sba_decode_kernel, pages_per_step=pages_per_step, heads=heads, hd=hd)
    grid_spec = pltpu.PrefetchScalarGridSpec(
        num_scalar_prefetch=1,
        grid=(n_seq, n_steps),
        in_specs=[pl.BlockSpec((heads, page), lambda b, c, pt: (0, 0)),
                  pl.BlockSpec((None, 1, w), lambda b, c, pt: (b, 0, 0))]
                 + [page_spec(j) for j in range(pages_per_step)] * 1
                 + [page_spec(j) for j in range(pages_per_step)]
                 + [pl.BlockSpec((page, page), lambda b, c, pt: (0, 0))],
        out_specs=pl.BlockSpec((None, 1, w), lambda b, c, pt: (b, 0, 0)),
        scratch_shapes=[pltpu.VMEM((heads, w), F32), pltpu.VMEM((heads, page), F32)],
    )
    out = pl.pallas_call(
        kern,
        grid_spec=grid_spec,
        out_shape=jax.ShapeDtypeStruct((n_seq, 1, w), F32),
        compiler_params=_params(("parallel", "arbitrary")),
        name="sba_decode",
    )(page_table.reshape(-1), bias_col, q.reshape(n_seq, 1, w),
      *([ck] * pages_per_step), *([cv] * pages_per_step), u)
    return out.reshape(n_seq, w)


def _gla_step_kernel(q_ref, k_ref, g_ref, v_ref, r_ref, ng_ref, s_ref, o_ref, sn_ref):
    s_new = jnp.exp(g_ref[...]) * s_ref[...] + k_ref[...] * v_ref[...]
    sn_ref[...] = s_new
    o = jnp.sum(q_ref[...] * s_new, axis=2, keepdims=True)
    o_ref[...] = _gla_gate_out(o, r_ref[...], ng_ref[...]).astype(o_ref.dtype)


def _gla_step(q_col, k_col, g_col, v_row, r_row, norm_g, state, *, bb=8):
    n_seq, heads, dk, dv = state.shape
    col = pl.BlockSpec((bb, heads, dk, 1), lambda i: (i, 0, 0, 0))
    rowv = pl.BlockSpec((bb, heads, 1, dv), lambda i: (i, 0, 0, 0))
    st = pl.BlockSpec((bb, heads, dk, dv), lambda i: (i, 0, 0, 0))
    return pl.pallas_call(
        _gla_step_kernel,
        grid=(n_seq // bb,),
        in_specs=[col, col, col, rowv, rowv, pl.BlockSpec((1, heads, 1, dv), lambda i: (0, 0, 0, 0)), st],
        out_specs=[rowv, st],
        out_shape=[jax.ShapeDtypeStruct((n_seq, heads, 1, dv), BF16),
                   jax.ShapeDtypeStruct(state.shape, F32)],
        compiler_params=_params(("parallel",)),
        name="gla_step",
    )(q_col, k_col, g_col, v_row, r_row, norm_g, state)


def _mem_decode_kernel(q_ref, k_ref, v_ref, o_ref, *, heads, hd):
    w = heads * hd
    head_of_lane = lax.broadcasted_iota(jnp.int32, (heads, w), 1) // hd
    own = head_of_lane == lax.broadcasted_iota(jnp.int32, (heads, w), 0)
    q_bd = jnp.where(own, jnp.broadcast_to(q_ref[...].astype(F32), (heads, w)), 0.0)
    s = lax.dot_general(q_bd, k_ref[...], (((1,), (1,)), ((), ())), preferred_element_type=F32)
    p = _softmax_rows(s)
    o = jnp.dot(p, v_ref[...], preferred_element_type=F32)
    o_ref[...] = jnp.sum(jnp.where(own, o, 0.0), axis=0, keepdims=True).astype(o_ref.dtype)


def _mem_decode(mq, mem_k, mem_v):
    n_seq, n_mem, heads, hd = mem_k.shape
    w = heads * hd
    kern = functools.partial(_mem_decode_kernel, heads=heads, hd=hd)
    vec = pl.BlockSpec((None, 1, w), lambda b: (b, 0, 0))
    kv = pl.BlockSpec((None, n_mem, w), lambda b: (b, 0, 0))
    out = pl.pallas_call(
        kern,
        grid=(n_seq,),
        in_specs=[vec, kv, kv],
        out_specs=vec,
        out_shape=jax.ShapeDtypeStruct((n_seq, 1, w), BF16),
        compiler_params=_params(("parallel",)),
        name="mem_decode",
    )(mq.reshape(n_seq, 1, w), mem_k.reshape(n_seq, n_mem, w), mem_v.reshape(n_seq, n_mem, w))
    return out.reshape(n_seq, w)


def kernel(x_prompt, x_sample, cache_sba_k, cache_sba_v, state_gla, cache_mem_k, cache_mem_v, page_table,
           mem_prompt, w_in, sba_bias, w_gate_up, b_gate, gla_norm_g, w_mem_k, w_mem_v, w_o, ln1_g, ln1_b,
           w_up, w_down, ln2_g, ln2_b):
    depth = w_in.shape[0]
    bp, tp, d = x_prompt.shape
    bs, ts, _ = x_sample.shape
    assert ts == 1, "the decode path handles one new token per sequence"
    _, n_phys, page, sba_h, sba_d = cache_sba_k.shape
    _, _, gla_h, gla_k, gla_v = state_gla.shape
    _, _, n_mem, mem_h, mem_d = cache_mem_k.shape
    rank = w_gate_up.shape[1]
    assert gla_v == LANES and gla_k <= LANES and sba_d == LANES and mem_d == LANES
    sba_w, gkw, gvw, mem_w = sba_h * sba_d, gla_h * gla_k, gla_h * gla_v, mem_h * mem_d
    gp_w = gla_h * LANES
    alpha = (2.0 * depth) ** 0.25
    dims = (sba_w, gkw, gvw, rank, mem_w, gla_h, gla_k)
    proj = functools.partial(_project, sba_w=sba_w, gp_w=gp_w, gvw=gvw, mem_w=mem_w,
                             sba_d=sba_d, gla_k=gla_k, mem_d=mem_d)

    yp = x_prompt.reshape(bp * tp, d)
    ys = x_sample.reshape(bs * ts, d)
    outs = [[] for _ in range(8)]
    for l in range(depth):
        w_packed, wg, bg = _pack_w_in(w_in[l], w_gate_up[l], b_gate[l], dims)
        wo = w_o[l].astype(BF16)
        wu = w_up[l].astype(BF16)
        wd = w_down[l].astype(BF16)
        ng = gla_norm_g[l].reshape(1, gvw)
        l1g, l1b = ln1_g[l].reshape(1, d), ln1_b[l].reshape(1, d)
        l2g, l2b = ln2_g[l].reshape(1, d), ln2_b[l].reshape(1, d)
        bias = sba_bias[l].astype(F32)

        q, kf, vf, kb, vb, gq, gk, gv, gr, gg, mq = proj(yp, w_packed, wg, bg, tm=256)
        o_sba = _sba_prompt(q, kb, vb, bias, batch=bp, seq=tp, heads=sba_h, hd=sba_d)
        og, s_p = _gla_prompt(gq, gk, gv, gg, gr, ng, batch=bp, seq=tp, heads=gla_h, dk=gla_k)
        mkf, mvf, mkb, mvb = _mem_kv(mem_prompt.reshape(bp * n_mem, d), w_mem_k[l].astype(BF16),
                                     w_mem_v[l].astype(BF16))
        x1 = _wo_ln(yp, o_sba, og, (mq, mkb, mvb), wo, l1g, l1b, tm=512, alpha=alpha, mem_heads=mem_h, hd=mem_d,
                    rows_per_batch=tp)
        yp_next = _mlp_ln(x1, wu, wd, l2g, l2b, tm=512, tf=1024, alpha=alpha)
        outs[0].append(kf.reshape(bp, tp, sba_h, sba_d))
        outs[1].append(vf.reshape(bp, tp, sba_h, sba_d))
        outs[2].append(s_p)
        outs[3].append(mkf.reshape(bp, n_mem, mem_h, mem_d))
        outs[4].append(mvf.reshape(bp, n_mem, mem_h, mem_d))

        q2, kf2, vf2, _, _, gq2, gk2, gv2, gr2, gg2, mq2 = proj(ys, w_packed, wg, bg, tm=bs)
        o_sba2 = _sba_decode(q2, cache_sba_k[l], cache_sba_v[l], page_table, bias).astype(BF16)
        col = lambda a: a.reshape(bs, gla_h, LANES)[:, :, :gla_k].reshape(bs, gla_h, gla_k, 1)
        rowv = lambda a: a.reshape(bs, gla_h, 1, gla_v)
        og2, s_s = _gla_step(col(gq2), col(gk2), col(gg2), rowv(gv2), rowv(gr2), ng.reshape(1, gla_h, 1, gla_v),
                             state_gla[l].astype(F32))
        o_mem2 = _mem_decode(mq2, cache_mem_k[l], cache_mem_v[l])
        x1s = _wo_ln(ys, o_sba2, og2.reshape(bs, gvw), (o_mem2,), wo, l1g, l1b, tm=bs, alpha=alpha,
                     mem_heads=mem_h, hd=mem_d)
        ys_next = _mlp_ln(x1s, wu, wd, l2g, l2b, tm=bs, tf=1024, alpha=alpha)
        outs[5].append(kf2.reshape(bs, ts, sba_h, sba_d))
        outs[6].append(vf2.reshape(bs, ts, sba_h, sba_d))
        outs[7].append(s_s)

        yp, ys = yp_next, ys_next

    return (yp.reshape(bp, tp, d), ys.reshape(bs, ts, d), *[jnp.stack(o) for o in outs])
```

```python
import functools
import math

import jax
import jax.numpy as jnp
from jax import lax
from jax.experimental import pallas as pl
from jax.experimental.pallas import tpu as pltpu

F32 = jnp.float32
BF16 = jnp.bfloat16

LANES = 128
LN_EPS = 1e-5
NORM_EPS = 1e-6
GLA_TAU = 16.0
LOG2E = math.log2(math.e)
LN2 = math.log(2.0)
VMEM_LIMIT_BYTES = 56 * 1024 * 1024


def _params(sem):
    return pltpu.CompilerParams(dimension_semantics=sem, vmem_limit_bytes=VMEM_LIMIT_BYTES)


def _softplus(z):
    return jnp.maximum(z, 0.0) + jnp.log1p(jnp.exp(-jnp.abs(z)))


def _layer_norm(v, g, b):
    mu = jnp.mean(v, axis=-1, keepdims=True)
    vc = v - mu
    var = jnp.mean(vc * vc, axis=-1, keepdims=True)
    return vc * lax.rsqrt(var + LN_EPS) * g + b


def _pack_w_in(w_in, w_gate_up, b_gate, dims):
    sba_w, gkw, gvw, rank, mem_w, gh, gk = dims
    o = [0]
    for s in (sba_w, sba_w, sba_w, gkw, gkw, gvw, gvw, rank, mem_w):
        o.append(o[-1] + s)
    d = w_in.shape[0]

    def pad_heads(w):
        w = w.reshape(d, gh, gk)
        return jnp.pad(w, ((0, 0), (0, 0), (0, LANES - gk))).reshape(d, gh * LANES)

    cols = [w_in[:, o[0]:o[3]], pad_heads(w_in[:, o[3]:o[4]]), pad_heads(w_in[:, o[4]:o[5]]),
            w_in[:, o[5]:o[7]], w_in[:, o[8]:o[9]],
            jnp.pad(w_in[:, o[7]:o[8]], ((0, 0), (0, LANES - rank)))]
    w_packed = jnp.concatenate(cols, axis=1).astype(BF16)
    wg = jnp.pad(w_gate_up.reshape(rank, gh, gk), ((0, LANES - rank), (0, 0), (0, LANES - gk)))
    wg = wg.reshape(LANES, gh * LANES).astype(BF16)
    bg = jnp.pad(b_gate.reshape(1, gh, gk), ((0, 0), (0, 0), (0, LANES - gk))).reshape(1, gh * LANES)
    return w_packed, wg, bg.astype(F32)


def _proj_kernel(x_ref, w_ref, wg_ref, bg_ref,
                 q_ref, kf_ref, vf_ref, kb_ref, vb_ref, gq_ref, gk_ref, gv_ref, gr_ref, gg_ref, mq_ref,
                 *, sba_w, gp_w, gvw, mem_w, sba_scale, gla_scale, mem_scale):
    xb = x_ref[...].astype(BF16)

    def mm(lo, width):
        return jnp.dot(xb, w_ref[:, lo:lo + width], preferred_element_type=F32)

    c = 0
    q_ref[...] = (mm(c, sba_w) * sba_scale).astype(BF16); c += sba_w
    k = mm(c, sba_w); c += sba_w
    kf_ref[...] = k
    kb_ref[...] = k.astype(BF16)
    v = mm(c, sba_w); c += sba_w
    vf_ref[...] = v
    vb_ref[...] = v.astype(BF16)
    gq_ref[...] = mm(c, gp_w) * gla_scale; c += gp_w
    gk_ref[...] = mm(c, gp_w); c += gp_w
    gv_ref[...] = mm(c, gvw); c += gvw
    gr_ref[...] = mm(c, gvw); c += gvw
    mq_ref[...] = (mm(c, mem_w) * mem_scale).astype(BF16); c += mem_w
    glow = mm(c, LANES)
    pre = jnp.dot(glow.astype(BF16), wg_ref[...], preferred_element_type=F32) + bg_ref[...]
    gg_ref[...] = -_softplus(-pre) * (1.0 / GLA_TAU)


def _project(x2d, w_packed, wg, bg, *, tm, sba_w, gp_w, gvw, mem_w, sba_d, gla_k, mem_d):
    m, d = x2d.shape
    npk = w_packed.shape[1]
    kern = functools.partial(_proj_kernel, sba_w=sba_w, gp_w=gp_w, gvw=gvw, mem_w=mem_w,
                             sba_scale=sba_d ** -0.5 * LOG2E, gla_scale=gla_k ** -0.5, mem_scale=mem_d ** -0.5)
    row = lambda w: pl.BlockSpec((tm, w), lambda i: (i, 0))
    const = lambda shape: pl.BlockSpec(shape, lambda i: (0, 0), pipeline_mode=pl.Buffered(1))
    out_shapes = [(sba_w, BF16), (sba_w, F32), (sba_w, F32), (sba_w, BF16), (sba_w, BF16),
                  (gp_w, F32), (gp_w, F32), (gvw, F32), (gvw, F32), (gp_w, F32), (mem_w, BF16)]
    return pl.pallas_call(
        kern,
        grid=(m // tm,),
        in_specs=[row(d), const((d, npk)), const(wg.shape), const(bg.shape)],
        out_specs=[row(w) for w, _ in out_shapes],
        out_shape=[jax.ShapeDtypeStruct((m, w), dt) for w, dt in out_shapes],
        compiler_params=_params(("parallel",)),
        name="in_proj",
    )(x2d, w_packed, wg, bg)


def _sba_tile(q_ref, k_ref, v_ref, u_ref, acc_ref, run_ref, bias, h, start, *, tq, hd, diag):
    lanes = slice(h * hd, (h + 1) * hd)
    k = k_ref[pl.ds(start, tq), lanes]
    v = v_ref[pl.ds(start, tq), lanes]
    z = lax.dot_general(q_ref[:, lanes], k, (((1,), (1,)), ((), ())), preferred_element_type=F32) + bias
    neg_abs = lax.bitcast_convert_type(lax.bitcast_convert_type(z, jnp.uint32) | jnp.uint32(0x80000000), F32)
    sp = jnp.maximum(z, 0.0) + jnp.log2(1.0 + jnp.exp2(neg_abs))
    log_beta = z - sp
    if diag:
        keep = lax.broadcasted_iota(jnp.int32, (tq, tq), 1) < lax.broadcasted_iota(jnp.int32, (tq, tq), 0)
        sp = jnp.where(keep, sp, 0.0)
        log_beta = jnp.where(keep, log_beta, -1e30)
    ext = jnp.dot(sp.astype(BF16), u_ref[...], preferred_element_type=F32)
    rest, total = ext[:, :tq], ext[:, tq:]
    if diag:
        w = jnp.exp2(log_beta - rest)
        acc_ref[:, lanes] = jnp.dot(w.astype(BF16), v, preferred_element_type=F32)
        run_ref[h] = total
    else:
        run = run_ref[h]
        w = jnp.exp2(log_beta - rest - jnp.concatenate([run] * (tq // LANES), axis=1))
        acc_ref[:, lanes] += jnp.dot(w.astype(BF16), v, preferred_element_type=F32)
        run_ref[h] = run + total


def _sba_prompt_kernel(bias_ref, q_ref, k_ref, v_ref, u_ref, o_ref, acc_ref, run_ref, *, tq, hd, heads_per_step):
    i = pl.program_id(2)
    hg = pl.program_id(1)
    tile = functools.partial(_sba_tile, q_ref, k_ref, v_ref, u_ref, acc_ref, run_ref, tq=tq, hd=hd)
    biases = [bias_ref[hg * heads_per_step + h] for h in range(heads_per_step)]

    for h in range(heads_per_step):
        tile(biases[h], h, pl.multiple_of(i * tq, tq), diag=True)

    def body(jj, carry):
        start = pl.multiple_of((i - 1 - jj) * tq, tq)
        for h in range(heads_per_step):
            tile(biases[h], h, start, diag=False)
        return carry

    lax.fori_loop(0, i, body, 0)
    o_ref[...] = acc_ref[...].astype(o_ref.dtype)


def _sba_prompt(q, k, v, bias, *, batch, seq, heads, hd, tq=256, heads_per_step=8):
    m = q.shape[0]
    nq = seq // tq
    w = heads_per_step * hd
    strict = lax.broadcasted_iota(jnp.int32, (tq, tq), 0) > lax.broadcasted_iota(jnp.int32, (tq, tq), 1)
    u = jnp.concatenate([strict.astype(BF16), jnp.ones((tq, LANES), BF16)], axis=1)
    kern = functools.partial(_sba_prompt_kernel, tq=tq, hd=hd, heads_per_step=heads_per_step)
    return pl.pallas_call(
        kern,
        grid=(batch, heads // heads_per_step, nq),
        in_specs=[pl.BlockSpec(memory_space=pltpu.SMEM),
                  pl.BlockSpec((tq, w), lambda b, h, i: (b * nq + i, h)),
                  pl.BlockSpec((seq, w), lambda b, h, i: (b, h)),
                  pl.BlockSpec((seq, w), lambda b, h, i: (b, h)),
                  pl.BlockSpec(u.shape, lambda b, h, i: (0, 0))],
        out_specs=pl.BlockSpec((tq, w), lambda b, h, i: (b * nq + i, h)),
        out_shape=jax.ShapeDtypeStruct((m, heads * hd), BF16),
        scratch_shapes=[pltpu.VMEM((tq, w), F32), pltpu.VMEM((heads_per_step, tq, LANES), F32)],
        compiler_params=_params(("parallel", "parallel", "arbitrary")),
        name="sba_prompt",
    )(bias, q, k, v, u)


def _gla_gate_out(o, r, g):
    og = o * lax.rsqrt(jnp.mean(o * o, axis=-1, keepdims=True) + NORM_EPS)
    return og * g * (r * jax.nn.sigmoid(r))


GLA_FACTORISED_MAX_DECAY = 80.0


def _gla_chunk(refs, rows, st, tri, *, exact, intra_ref=None):
    q_ref, k_ref, v_ref, g_ref, r_ref, ng_ref, o_ref = refs
    chunk = tri.shape[0]
    q = q_ref[rows, :]
    k = k_ref[rows, :]
    v = v_ref[rows, :]
    gc = jnp.dot(tri, g_ref[rows, :], preferred_element_type=F32, precision=lax.Precision.HIGHEST)
    g_last = gc[chunk - 1:chunk, :]
    inter = lax.dot_general((q * jnp.exp(gc)).astype(BF16), st.astype(BF16),
                            (((1,), (1,)), ((), ())), preferred_element_type=F32)
    if not exact:
        g_mid = gc[chunk // 2 - 1:chunk // 2, :]
        qi = (q * jnp.exp(gc - g_mid)).astype(BF16)
        ki = (k * jnp.exp(g_mid - gc)).astype(BF16)
        scores = lax.dot_general(qi, ki, (((1,), (1,)), ((), ())), preferred_element_type=F32)
        scores = jnp.where(tri > 0.5, scores, 0.0)
        intra = jnp.dot(scores.astype(BF16), v.astype(BF16), preferred_element_type=F32)
    else:
        row_id = lax.broadcasted_iota(jnp.int32, (chunk, 1), 0)

        def row(i, carry):
            sel = row_id == i
            qrow = jnp.sum(jnp.where(sel, q, 0.0), axis=0, keepdims=True)
            grow = jnp.sum(jnp.where(sel, gc, 0.0), axis=0, keepdims=True)
            decay = jnp.exp(jnp.minimum(grow - gc, 0.0))
            s = jnp.sum(qrow * k * decay, axis=-1, keepdims=True)
            s = jnp.where(row_id <= i, s, 0.0)
            intra_ref[pl.ds(i, 1), :] = jnp.sum(s * v, axis=0, keepdims=True)
            return carry

        lax.fori_loop(0, chunk, row, 0)
        intra = intra_ref[...]
    o_ref[rows, :] = _gla_gate_out(inter + intra, r_ref[rows, :], ng_ref[...]).astype(o_ref.dtype)
    kd = (k * jnp.exp(g_last - gc)).astype(BF16)
    return st * jnp.exp(g_last) + lax.dot_general(v.astype(BF16), kd, (((0,), (0,)), ((), ())),
                                                  preferred_element_type=F32)


def _gla_prompt_kernel(q_ref, k_ref, v_ref, g_ref, r_ref, ng_ref, tri_ref, o_ref, s_ref, st_ref, intra_ref,
                       *, chunk, n_chunks, dk):
    refs = (q_ref, k_ref, v_ref, g_ref, r_ref, ng_ref, o_ref)
    tri = tri_ref[...]
    chunk_decay = jnp.sum(g_ref[...].reshape(n_chunks, chunk, LANES), axis=1)
    factorised_ok = jnp.min(chunk_decay) >= -GLA_FACTORISED_MAX_DECAY

    @pl.when(factorised_ok)
    def _():
        st = jnp.zeros((LANES, LANES), F32)
        for c in range(n_chunks):
            st = _gla_chunk(refs, slice(c * chunk, (c + 1) * chunk), st, tri, exact=False)
        st_ref[...] = st

    @pl.when(jnp.logical_not(factorised_ok))
    def _():
        def body(c, st):
            rows = pl.ds(pl.multiple_of(c * chunk, chunk), chunk)
            return _gla_chunk(refs, rows, st, tri, exact=True, intra_ref=intra_ref)
        st_ref[...] = lax.fori_loop(0, n_chunks, body, jnp.zeros((LANES, LANES), F32))

    s_ref[...] = st_ref[...].T[:dk, :]


def _gla_prompt(gq, gk, gv, gg, gr, norm_g, *, batch, seq, heads, dk, chunk=128):
    m = gq.shape[0]
    n_chunks = seq // chunk
    tri = (lax.broadcasted_iota(jnp.int32, (chunk, chunk), 0)
           >= lax.broadcasted_iota(jnp.int32, (chunk, chunk), 1)).astype(F32)
    kern = functools.partial(_gla_prompt_kernel, chunk=chunk, n_chunks=n_chunks, dk=dk)
    blk = pl.BlockSpec((seq, LANES), lambda b, h: (b, h))
    return pl.pallas_call(
        kern,
        grid=(batch, heads),
        in_specs=[blk, blk, blk, blk, blk,
                  pl.BlockSpec((1, LANES), lambda b, h: (0, h)),
                  pl.BlockSpec((chunk, chunk), lambda b, h: (0, 0))],
        out_specs=[blk, pl.BlockSpec((None, None, dk, LANES), lambda b, h: (b, h, 0, 0))],
        out_shape=[jax.ShapeDtypeStruct((m, heads * LANES), BF16),
                   jax.ShapeDtypeStruct((batch, heads, dk, LANES), F32)],
        scratch_shapes=[pltpu.VMEM((LANES, LANES), F32), pltpu.VMEM((chunk, LANES), F32)],
        compiler_params=_params(("parallel", "parallel")),
        name="gla_prompt",
    )(gq, gk, gv, gg, gr, norm_g, tri)


def _mem_kv_kernel(x_ref, wk_ref, wv_ref, kf_ref, vf_ref, kb_ref, vb_ref):
    xb = x_ref[...].astype(BF16)
    k = jnp.dot(xb, wk_ref[...], preferred_element_type=F32)
    v = jnp.dot(xb, wv_ref[...], preferred_element_type=F32)
    kf_ref[...] = k
    vf_ref[...] = v
    kb_ref[...] = k.astype(BF16)
    vb_ref[...] = v.astype(BF16)


def _mem_kv(mem2d, wk, wv, *, tm=512):
    m, d = mem2d.shape
    w = wk.shape[1]
    row = lambda width: pl.BlockSpec((tm, width), lambda i: (i, 0))
    const = pl.BlockSpec((d, w), lambda i: (0, 0))
    return pl.pallas_call(
        _mem_kv_kernel,
        grid=(m // tm,),
        in_specs=[row(d), const, const],
        out_specs=[row(w)] * 4,
        out_shape=[jax.ShapeDtypeStruct((m, w), F32)] * 2 + [jax.ShapeDtypeStruct((m, w), BF16)] * 2,
        compiler_params=_params(("parallel",)),
        name="mem_kv",
    )(mem2d, wk, wv)


def _softmax_rows(s):
    s = s - jnp.max(s, axis=-1, keepdims=True)
    p = jnp.exp(s)
    return p / jnp.sum(p, axis=-1, keepdims=True)


def _wo_ln_kernel(*refs, alpha, sba_w, gvw, mem_heads, hd, fuse_mem):
    if fuse_mem:
        x_ref, a_ref, g_ref, mq_ref, mk_ref, mv_ref, wo_ref, lg_ref, lb_ref, o_ref = refs
    else:
        x_ref, a_ref, g_ref, om_ref, wo_ref, lg_ref, lb_ref, o_ref = refs
    acc = alpha * x_ref[...]
    acc = acc + jnp.dot(a_ref[...], wo_ref[0:sba_w, :], preferred_element_type=F32)
    acc = acc + jnp.dot(g_ref[...], wo_ref[sba_w:sba_w + gvw, :], preferred_element_type=F32)
    base = sba_w + gvw
    if fuse_mem:
        for h in range(mem_heads):
            lanes = slice(h * hd, (h + 1) * hd)
            s = lax.dot_general(mq_ref[:, lanes], mk_ref[:, lanes], (((1,), (1,)), ((), ())),
                                preferred_element_type=F32)
            p = _softmax_rows(s)
            om = jnp.dot(p.astype(BF16), mv_ref[:, lanes], preferred_element_type=F32)
            acc = acc + jnp.dot(om.astype(BF16), wo_ref[base + h * hd:base + (h + 1) * hd, :],
                                preferred_element_type=F32)
    else:
        acc = acc + jnp.dot(om_ref[...], wo_ref[base:, :], preferred_element_type=F32)
    o_ref[...] = _layer_norm(acc, lg_ref[...], lb_ref[...])


def _wo_ln(x2d, o_sba, og, mem_args, wo, ln_g, ln_b, *, tm, alpha, mem_heads, hd, rows_per_batch=None):
    m, d = x2d.shape
    sba_w, gvw = o_sba.shape[1], og.shape[1]
    fuse_mem = rows_per_batch is not None
    row = lambda w: pl.BlockSpec((tm, w), lambda i: (i, 0))
    const = lambda shape: pl.BlockSpec(shape, lambda i: (0, 0), pipeline_mode=pl.Buffered(1))
    if fuse_mem:
        mq, mk, mv = mem_args
        n_mem = mk.shape[0] // (m // rows_per_batch)
        tiles_per_batch = rows_per_batch // tm
        per_batch = pl.BlockSpec((n_mem, mk.shape[1]), lambda i: (i // tiles_per_batch, 0))
        mem_specs = [row(mq.shape[1]), per_batch, per_batch]
    else:
        mem_specs = [row(mem_args[0].shape[1])]
    kern = functools.partial(_wo_ln_kernel, alpha=alpha, sba_w=sba_w, gvw=gvw, mem_heads=mem_heads, hd=hd,
                             fuse_mem=fuse_mem)
    return pl.pallas_call(
        kern,
        grid=(m // tm,),
        in_specs=[row(d), row(sba_w), row(gvw)] + mem_specs + [const(wo.shape), const(ln_g.shape), const(ln_b.shape)],
        out_specs=row(d),
        out_shape=jax.ShapeDtypeStruct((m, d), F32),
        compiler_params=_params(("parallel",)),
        name="wo_ln1",
    )(x2d, o_sba, og, *mem_args, wo, ln_g, ln_b)


def _mlp_kernel(x_ref, wu_ref, wd_ref, lg_ref, lb_ref, o_ref, xb_ref, acc_ref, *, alpha):
    f = pl.program_id(1)

    @pl.when(f == 0)
    def _():
        xb_ref[...] = x_ref[...].astype(BF16)
        acc_ref[...] = jnp.zeros_like(acc_ref)

    h = jnp.dot(xb_ref[...], wu_ref[...], preferred_element_type=F32)
    u = jnp.square(jnp.maximum(h, 0.0))
    acc_ref[...] += jnp.dot(u.astype(BF16), wd_ref[...], preferred_element_type=F32)

    @pl.when(f == pl.num_programs(1) - 1)
    def _():
        o_ref[...] = _layer_norm(alpha * x_ref[...] + acc_ref[...], lg_ref[...], lb_ref[...])


def _mlp_ln(x1, wu, wd, ln_g, ln_b, *, tm, tf, alpha):
    m, d = x1.shape
    dff = wu.shape[1]
    kern = functools.partial(_mlp_kernel, alpha=alpha)
    return pl.pallas_call(
        kern,
        grid=(m // tm, dff // tf),
        in_specs=[pl.BlockSpec((tm, d), lambda i, f: (i, 0)),
                  pl.BlockSpec((d, tf), lambda i, f: (0, f)),
                  pl.BlockSpec((tf, d), lambda i, f: (f, 0)),
                  pl.BlockSpec((1, d), lambda i, f: (0, 0)),
                  pl.BlockSpec((1, d), lambda i, f: (0, 0))],
        out_specs=pl.BlockSpec((tm, d), lambda i, f: (i, 0)),
        out_shape=jax.ShapeDtypeStruct((m, d), F32),
        scratch_shapes=[pltpu.VMEM((tm, d), BF16), pltpu.VMEM((tm, d), F32)],
        compiler_params=_params(("parallel", "arbitrary")),
        name="mlp_ln2",
    )(x1, wu, wd, ln_g, ln_b)


def _sba_decode_kernel(pt_ref, bias_ref, q_ref, *refs, pages_per_step, heads, hd):
    k_refs = refs[:pages_per_step]
    v_refs = refs[pages_per_step:2 * pages_per_step]
    u_ref = refs[2 * pages_per_step]
    o_ref = refs[2 * pages_per_step + 1]
    acc_ref, run_ref = refs[2 * pages_per_step + 2:]
    c = pl.program_id(1)
    w = heads * hd

    @pl.when(c == 0)
    def _():
        acc_ref[...] = jnp.zeros_like(acc_ref)
        run_ref[...] = jnp.zeros_like(run_ref)

    head_of_lane = lax.broadcasted_iota(jnp.int32, (heads, w), 1) // hd
    own = head_of_lane == lax.broadcasted_iota(jnp.int32, (heads, w), 0)
    q_bd = jnp.where(own, jnp.broadcast_to(q_ref[...].astype(F32), (heads, w)), 0.0)
    page = u_ref.shape[0]
    z = jnp.concatenate(
        [lax.dot_general(q_bd, k_refs[j][...], (((1,), (1,)), ((), ())), preferred_element_type=F32)
         for j in range(pages_per_step)], axis=0)
    z = (z + bias_ref[...]) * LN2
    sp = _softplus(z)
    log_keep = -sp
    log_beta = z - sp
    ext = jnp.dot(log_keep, u_ref[...], preferred_element_type=F32, precision=lax.Precision.HIGHEST)
    incl, total = ext[:, :page], ext[:, page:]
    run = run_ref[...]
    offs = []
    for j in range(pages_per_step):
        offs.append(run)
        run = run + total[j * heads:(j + 1) * heads]
    run_ref[...] = run
    wgt = jnp.exp(log_beta + (incl - log_keep) + jnp.concatenate(offs, axis=0))
    acc = acc_ref[...]
    for j in range(pages_per_step):
        acc = acc + jnp.dot(wgt[j * heads:(j + 1) * heads], v_refs[j][...], preferred_element_type=F32)
    acc_ref[...] = acc

    @pl.when(c == pl.num_programs(1) - 1)
    def _():
        o_ref[...] = jnp.sum(jnp.where(own, acc, 0.0), axis=0, keepdims=True)


def _sba_decode(q, cache_k, cache_v, layer, page_table, bias, *, pages_per_step=16):
    n_seq, n_pages = page_table.shape
    depth, n_phys, page, heads, hd = cache_k.shape
    w = heads * hd
    n_steps = n_pages // pages_per_step
    ck = cache_k.reshape(depth * n_phys, page, w)
    cv = cache_v.reshape(depth * n_phys, page, w)
    tri = lax.broadcasted_iota(jnp.int32, (page, page), 0) >= lax.broadcasted_iota(jnp.int32, (page, page), 1)
    u = jnp.concatenate([tri.astype(F32), jnp.ones((page, page), F32)], axis=1)
    bias_rows = jnp.tile(jnp.broadcast_to(bias.reshape(heads, 1), (heads, page)), (pages_per_step, 1)).astype(F32)

    def page_spec(j):
        def index(b, c, pt):
            logical = n_pages - 1 - (c * pages_per_step + j)
            return (layer * n_phys + pt[b * n_pages + logical], 0, 0)
        return pl.BlockSpec((None, page, w), index)

    kern = functools.partial(_sba_decode_kernel, pages_per_step=pages_per_step, heads=heads, hd=hd)
    grid_spec = pltpu.PrefetchScalarGridSpec(
        num_scalar_prefetch=1,
        grid=(n_seq, n_steps),
        in_specs=[pl.BlockSpec(bias_rows.shape, lambda b, c, pt: (0, 0)),
                  pl.BlockSpec((None, 1, w), lambda b, c, pt: (b, 0, 0))]
                 + [page_spec(j) for j in range(pages_per_step)]
                 + [page_spec(j) for j in range(pages_per_step)]
                 + [pl.BlockSpec(u.shape, lambda b, c, pt: (0, 0))],
        out_specs=pl.BlockSpec((None, 1, w), lambda b, c, pt: (b, 0, 0)),
        scratch_shapes=[pltpu.VMEM((heads, w), F32), pltpu.VMEM((heads, page), F32)],
    )
    out = pl.pallas_call(
        kern,
        grid_spec=grid_spec,
        out_shape=jax.ShapeDtypeStruct((n_seq, 1, w), F32),
        compiler_params=_params(("parallel", "arbitrary")),
        name="sba_decode",
    )(page_table.reshape(-1), bias_rows, q.reshape(n_seq, 1, w),
      *([ck] * pages_per_step), *([cv] * pages_per_step), u)
    return out.reshape(n_seq, w)


def _gla_step_kernel(q_ref, k_ref, g_ref, v_ref, r_ref, ng_ref, s_ref, o_ref, sn_ref):
    s_new = jnp.exp(g_ref[...]) * s_ref[...].astype(F32) + k_ref[...] * v_ref[...]
    sn_ref[...] = s_new
    o = jnp.sum(q_ref[...] * s_new, axis=2, keepdims=True)
    o_ref[...] = _gla_gate_out(o, r_ref[...], ng_ref[...]).astype(o_ref.dtype)


def _gla_step(q_col, k_col, g_col, v_row, r_row, norm_g, state, layer, *, bb=8):
    depth, n_seq, heads, dk, dv = state.shape
    steps = n_seq // bb
    col = pl.BlockSpec((bb, heads, dk, 1), lambda i: (i, 0, 0, 0))
    rowv = pl.BlockSpec((bb, heads, 1, dv), lambda i: (i, 0, 0, 0))
    st = pl.BlockSpec((bb, heads, dk, dv), lambda i: (i, 0, 0, 0))
    st_in = pl.BlockSpec((bb, heads, dk, dv), lambda i: (layer * steps + i, 0, 0, 0))
    return pl.pallas_call(
        _gla_step_kernel,
        grid=(steps,),
        in_specs=[col, col, col, rowv, rowv, pl.BlockSpec((1, heads, 1, dv), lambda i: (0, 0, 0, 0)), st_in],
        out_specs=[rowv, st],
        out_shape=[jax.ShapeDtypeStruct((n_seq, heads, 1, dv), BF16),
                   jax.ShapeDtypeStruct((n_seq, heads, dk, dv), F32)],
        compiler_params=_params(("parallel",)),
        name="gla_step",
    )(q_col, k_col, g_col, v_row, r_row, norm_g, state.reshape(depth * n_seq, heads, dk, dv))


def _mem_decode_kernel(q_ref, k_ref, v_ref, o_ref, *, heads, hd):
    w = heads * hd
    head_of_lane = lax.broadcasted_iota(jnp.int32, (heads, w), 1) // hd
    own = head_of_lane == lax.broadcasted_iota(jnp.int32, (heads, w), 0)
    q_bd = jnp.where(own, jnp.broadcast_to(q_ref[...].astype(F32), (heads, w)), 0.0)
    s = lax.dot_general(q_bd, k_ref[...], (((1,), (1,)), ((), ())), preferred_element_type=F32)
    p = _softmax_rows(s)
    o = jnp.dot(p, v_ref[...], preferred_element_type=F32)
    o_ref[...] = jnp.sum(jnp.where(own, o, 0.0), axis=0, keepdims=True).astype(o_ref.dtype)


def _mem_decode(mq, mem_k, mem_v, layer):
    depth, n_seq, n_mem, heads, hd = mem_k.shape
    w = heads * hd
    kern = functools.partial(_mem_decode_kernel, heads=heads, hd=hd)
    vec = pl.BlockSpec((None, 1, w), lambda b: (b, 0, 0))
    kv = pl.BlockSpec((None, n_mem, w), lambda b: (layer * n_seq + b, 0, 0))
    out = pl.pallas_call(
        kern,
        grid=(n_seq,),
        in_specs=[vec, kv, kv],
        out_specs=vec,
        out_shape=jax.ShapeDtypeStruct((n_seq, 1, w), BF16),
        compiler_params=_params(("parallel",)),
        name="mem_decode",
    )(mq.reshape(n_seq, 1, w), mem_k.reshape(depth * n_seq, n_mem, w), mem_v.reshape(depth * n_seq, n_mem, w))
    return out.reshape(n_seq, w)


def kernel(x_prompt, x_sample, cache_sba_k, cache_sba_v, state_gla, cache_mem_k, cache_mem_v, page_table,
           mem_prompt, w_in, sba_bias, w_gate_up, b_gate, gla_norm_g, w_mem_k, w_mem_v, w_o, ln1_g, ln1_b,
           w_up, w_down, ln2_g, ln2_b):
    depth = w_in.shape[0]
    bp, tp, d = x_prompt.shape
    bs, ts, _ = x_sample.shape
    assert ts == 1, "the decode path handles one new token per sequence"
    _, n_phys, page, sba_h, sba_d = cache_sba_k.shape
    _, _, gla_h, gla_k, gla_v = state_gla.shape
    _, _, n_mem, mem_h, mem_d = cache_mem_k.shape
    rank = w_gate_up.shape[1]
    assert gla_v == LANES and gla_k <= LANES and sba_d == LANES and mem_d == LANES
    sba_w, gkw, gvw, mem_w = sba_h * sba_d, gla_h * gla_k, gla_h * gla_v, mem_h * mem_d
    gp_w = gla_h * LANES
    alpha = (2.0 * depth) ** 0.25
    dims = (sba_w, gkw, gvw, rank, mem_w, gla_h, gla_k)
    proj = functools.partial(_project, sba_w=sba_w, gp_w=gp_w, gvw=gvw, mem_w=mem_w,
                             sba_d=sba_d, gla_k=gla_k, mem_d=mem_d)

    yp = x_prompt.reshape(bp * tp, d)
    ys = x_sample.reshape(bs * ts, d)
    outs = [[] for _ in range(8)]
    for l in range(depth):
        w_packed, wg, bg = _pack_w_in(w_in[l], w_gate_up[l], b_gate[l], dims)
        wo = w_o[l].astype(BF16)
        wu = w_up[l].astype(BF16)
        wd = w_down[l].astype(BF16)
        ng = gla_norm_g[l].reshape(1, gvw)
        l1g, l1b = ln1_g[l].reshape(1, d), ln1_b[l].reshape(1, d)
        l2g, l2b = ln2_g[l].reshape(1, d), ln2_b[l].reshape(1, d)
        bias = sba_bias[l].astype(F32)

        q, kf, vf, kb, vb, gq, gk, gv, gr, gg, mq = proj(yp, w_packed, wg, bg, tm=256)
        o_sba = _sba_prompt(q, kb, vb, bias * LOG2E, batch=bp, seq=tp, heads=sba_h, hd=sba_d)
        og, s_p = _gla_prompt(gq, gk, gv, gg, gr, ng, batch=bp, seq=tp, heads=gla_h, dk=gla_k)
        mkf, mvf, mkb, mvb = _mem_kv(mem_prompt.reshape(bp * n_mem, d), w_mem_k[l].astype(BF16),
                                     w_mem_v[l].astype(BF16))
        x1 = _wo_ln(yp, o_sba, og, (mq, mkb, mvb), wo, l1g, l1b, tm=512, alpha=alpha, mem_heads=mem_h, hd=mem_d,
                    rows_per_batch=tp)
        yp_next = _mlp_ln(x1, wu, wd, l2g, l2b, tm=512, tf=1024, alpha=alpha)
        outs[0].append(kf.reshape(bp, tp, sba_h, sba_d))
        outs[1].append(vf.reshape(bp, tp, sba_h, sba_d))
        outs[2].append(s_p)
        outs[3].append(mkf.reshape(bp, n_mem, mem_h, mem_d))
        outs[4].append(mvf.reshape(bp, n_mem, mem_h, mem_d))

        q2, kf2, vf2, _, _, gq2, gk2, gv2, gr2, gg2, mq2 = proj(ys, w_packed, wg, bg, tm=bs)
        o_sba2 = _sba_decode(q2, cache_sba_k, cache_sba_v, l, page_table, bias * LOG2E).astype(BF16)
        col = lambda a: a.reshape(bs, gla_h, LANES)[:, :, :gla_k].reshape(bs, gla_h, gla_k, 1)
        rowv = lambda a: a.reshape(bs, gla_h, 1, gla_v)
        og2, s_s = _gla_step(col(gq2), col(gk2), col(gg2), rowv(gv2), rowv(gr2), ng.reshape(1, gla_h, 1, gla_v),
                             state_gla, l)
        o_mem2 = _mem_decode(mq2, cache_mem_k, cache_mem_v, l)
        x1s = _wo_ln(ys, o_sba2, og2.reshape(bs, gvw), (o_mem2,), wo, l1g, l1b, tm=bs, alpha=alpha,
                     mem_heads=mem_h, hd=mem_d)
        ys_next = _mlp_ln(x1s, wu, wd, l2g, l2b, tm=bs, tf=1024, alpha=alpha)
        outs[5].append(kf2.reshape(bs, ts, sba_h, sba_d))
        outs[6].append(vf2.reshape(bs, ts, sba_h, sba_d))
        outs[7].append(s_s)

        yp, ys = yp_next, ys_next

    return (yp.reshape(bp, tp, d), ys.reshape(bs, ts, d), *[jnp.stack(o) for o in outs])
```

```python
import functools
import math

import jax
import jax.numpy as jnp
from jax import lax
from jax.experimental import pallas as pl
from jax.experimental.pallas import tpu as pltpu

F32 = jnp.float32
BF16 = jnp.bfloat16

LANES = 128
LN_EPS = 1e-5
NORM_EPS = 1e-6
GLA_TAU = 16.0
LOG2E = math.log2(math.e)
LN2 = math.log(2.0)
VMEM_LIMIT_BYTES = 56 * 1024 * 1024


def _params(sem):
    return pltpu.CompilerParams(dimension_semantics=sem, vmem_limit_bytes=VMEM_LIMIT_BYTES)


def _softplus(z):
    return jnp.maximum(z, 0.0) + jnp.log1p(jnp.exp(-jnp.abs(z)))


def _layer_norm(v, g, b):
    mu = jnp.mean(v, axis=-1, keepdims=True)
    vc = v - mu
    var = jnp.mean(vc * vc, axis=-1, keepdims=True)
    return vc * lax.rsqrt(var + LN_EPS) * g + b


def _pack_w_in(w_in, w_gate_up, b_gate, dims):
    sba_w, gkw, gvw, rank, mem_w, gh, gk = dims
    o = [0]
    for s in (sba_w, sba_w, sba_w, gkw, gkw, gvw, gvw, rank, mem_w):
        o.append(o[-1] + s)
    d = w_in.shape[0]

    def pad_heads(w):
        w = w.reshape(d, gh, gk)
        return jnp.pad(w, ((0, 0), (0, 0), (0, LANES - gk))).reshape(d, gh * LANES)

    cols = [w_in[:, o[0]:o[3]], pad_heads(w_in[:, o[3]:o[4]]), pad_heads(w_in[:, o[4]:o[5]]),
            w_in[:, o[5]:o[7]], w_in[:, o[8]:o[9]],
            jnp.pad(w_in[:, o[7]:o[8]], ((0, 0), (0, LANES - rank)))]
    w_packed = jnp.concatenate(cols, axis=1).astype(BF16)
    wg = jnp.pad(w_gate_up.reshape(rank, gh, gk), ((0, LANES - rank), (0, 0), (0, LANES - gk)))
    wg = wg.reshape(LANES, gh * LANES).astype(BF16)
    bg = jnp.pad(b_gate.reshape(1, gh, gk), ((0, 0), (0, 0), (0, LANES - gk))).reshape(1, gh * LANES)
    return w_packed, wg, bg.astype(F32)


def _proj_kernel(x_ref, w_ref, wg_ref, bg_ref,
                 q_ref, kf_ref, vf_ref, kb_ref, vb_ref, gq_ref, gk_ref, gv_ref, gr_ref, gg_ref, mq_ref,
                 *, sba_w, gp_w, gvw, mem_w, sba_scale, gla_scale, mem_scale):
    xb = x_ref[...].astype(BF16)

    def mm(lo, width):
        return jnp.dot(xb, w_ref[:, lo:lo + width], preferred_element_type=F32)

    c = 0
    q_ref[...] = (mm(c, sba_w) * sba_scale).astype(BF16); c += sba_w
    k = mm(c, sba_w); c += sba_w
    kf_ref[...] = k
    kb_ref[...] = k.astype(BF16)
    v = mm(c, sba_w); c += sba_w
    vf_ref[...] = v
    vb_ref[...] = v.astype(BF16)
    gq_ref[...] = mm(c, gp_w) * gla_scale; c += gp_w
    gk_ref[...] = mm(c, gp_w); c += gp_w
    gv_ref[...] = mm(c, gvw); c += gvw
    gr_ref[...] = mm(c, gvw); c += gvw
    mq_ref[...] = (mm(c, mem_w) * mem_scale).astype(BF16); c += mem_w
    glow = mm(c, LANES)
    pre = jnp.dot(glow.astype(BF16), wg_ref[...], preferred_element_type=F32) + bg_ref[...]
    gg_ref[...] = -_softplus(-pre) * (1.0 / GLA_TAU)


def _project(x2d, w_packed, wg, bg, *, tm, sba_w, gp_w, gvw, mem_w, sba_d, gla_k, mem_d):
    m, d = x2d.shape
    npk = w_packed.shape[1]
    kern = functools.partial(_proj_kernel, sba_w=sba_w, gp_w=gp_w, gvw=gvw, mem_w=mem_w,
                             sba_scale=sba_d ** -0.5 * LOG2E, gla_scale=gla_k ** -0.5, mem_scale=mem_d ** -0.5)
    row = lambda w: pl.BlockSpec((tm, w), lambda i: (i, 0))
    const = lambda shape: pl.BlockSpec(shape, lambda i: (0, 0), pipeline_mode=pl.Buffered(1))
    out_shapes = [(sba_w, BF16), (sba_w, F32), (sba_w, F32), (sba_w, BF16), (sba_w, BF16),
                  (gp_w, F32), (gp_w, F32), (gvw, F32), (gvw, F32), (gp_w, F32), (mem_w, BF16)]
    return pl.pallas_call(
        kern,
        grid=(m // tm,),
        in_specs=[row(d), const((d, npk)), const(wg.shape), const(bg.shape)],
        out_specs=[row(w) for w, _ in out_shapes],
        out_shape=[jax.ShapeDtypeStruct((m, w), dt) for w, dt in out_shapes],
        compiler_params=_params(("parallel",)),
        name="in_proj",
    )(x2d, w_packed, wg, bg)


def _sba_tile(q_ref, k_ref, v_ref, u_ref, acc_ref, run_ref, bias, h, start, *, tq, hd, diag):
    lanes = slice(h * hd, (h + 1) * hd)
    k = k_ref[pl.ds(start, tq), lanes]
    v = v_ref[pl.ds(start, tq), lanes]
    z = lax.dot_general(q_ref[:, lanes], k, (((1,), (1,)), ((), ())), preferred_element_type=F32) + bias
    sp = jnp.maximum(z, 0.0) + jnp.log2(1.0 + jnp.exp2(-jnp.abs(z)))
    log_beta = z - sp
    if diag:
        keep = lax.broadcasted_iota(jnp.int32, (tq, tq), 1) < lax.broadcasted_iota(jnp.int32, (tq, tq), 0)
        sp = jnp.where(keep, sp, 0.0)
        log_beta = jnp.where(keep, log_beta, -1e30)
    ext = jnp.dot(sp.astype(BF16), u_ref[...], preferred_element_type=F32)
    rest, total = ext[:, :tq], ext[:, tq:]
    if diag:
        w = jnp.exp2(log_beta - rest)
        acc_ref[:, lanes] = jnp.dot(w.astype(BF16), v, preferred_element_type=F32)
        run_ref[h] = total
    else:
        run = run_ref[h]
        w = jnp.exp2(log_beta - rest - jnp.concatenate([run] * (tq // LANES), axis=1))
        acc_ref[:, lanes] += jnp.dot(w.astype(BF16), v, preferred_element_type=F32)
        run_ref[h] = run + total


def _sba_prompt_kernel(bias_ref, q_ref, k_ref, v_ref, u_ref, o_ref, acc_ref, run_ref, *, tq, hd, heads_per_step):
    i = pl.program_id(2)
    hg = pl.program_id(1)
    tile = functools.partial(_sba_tile, q_ref, k_ref, v_ref, u_ref, acc_ref, run_ref, tq=tq, hd=hd)
    biases = [bias_ref[hg * heads_per_step + h] for h in range(heads_per_step)]

    for h in range(heads_per_step):
        tile(biases[h], h, pl.multiple_of(i * tq, tq), diag=True)

    def body(jj, carry):
        start = pl.multiple_of((i - 1 - jj) * tq, tq)
        for h in range(heads_per_step):
            tile(biases[h], h, start, diag=False)
        return carry

    lax.fori_loop(0, i, body, 0)
    o_ref[...] = acc_ref[...].astype(o_ref.dtype)


def _sba_prompt(q, k, v, bias, *, batch, seq, heads, hd, tq=256, heads_per_step=8):
    m = q.shape[0]
    nq = seq // tq
    w = heads_per_step * hd
    strict = lax.broadcasted_iota(jnp.int32, (tq, tq), 0) > lax.broadcasted_iota(jnp.int32, (tq, tq), 1)
    u = jnp.concatenate([strict.astype(BF16), jnp.ones((tq, LANES), BF16)], axis=1)
    kern = functools.partial(_sba_prompt_kernel, tq=tq, hd=hd, heads_per_step=heads_per_step)
    return pl.pallas_call(
        kern,
        grid=(batch, heads // heads_per_step, nq),
        in_specs=[pl.BlockSpec(memory_space=pltpu.SMEM),
                  pl.BlockSpec((tq, w), lambda b, h, i: (b * nq + i, h)),
                  pl.BlockSpec((seq, w), lambda b, h, i: (b, h)),
                  pl.BlockSpec((seq, w), lambda b, h, i: (b, h)),
                  pl.BlockSpec(u.shape, lambda b, h, i: (0, 0))],
        out_specs=pl.BlockSpec((tq, w), lambda b, h, i: (b * nq + i, h)),
        out_shape=jax.ShapeDtypeStruct((m, heads * hd), BF16),
        scratch_shapes=[pltpu.VMEM((tq, w), F32), pltpu.VMEM((heads_per_step, tq, LANES), F32)],
        compiler_params=_params(("parallel", "parallel", "arbitrary")),
        name="sba_prompt",
    )(bias, q, k, v, u)


def _gla_gate_out(o, r, g):
    og = o * lax.rsqrt(jnp.mean(o * o, axis=-1, keepdims=True) + NORM_EPS)
    return og * g * (r * jax.nn.sigmoid(r))


GLA_FACTORISED_MAX_DECAY = 80.0


def _gla_chunk(refs, rows, st, tri, *, exact, intra_ref=None):
    q_ref, k_ref, v_ref, g_ref, r_ref, ng_ref, o_ref = refs
    chunk = tri.shape[0]
    q = q_ref[rows, :]
    k = k_ref[rows, :]
    v = v_ref[rows, :]
    gc = jnp.dot(tri, g_ref[rows, :], preferred_element_type=F32, precision=lax.Precision.HIGHEST)
    g_last = gc[chunk - 1:chunk, :]
    inter = lax.dot_general((q * jnp.exp(gc)).astype(BF16), st.astype(BF16),
                            (((1,), (1,)), ((), ())), preferred_element_type=F32)
    if not exact:
        g_mid = gc[chunk // 2 - 1:chunk // 2, :]
        qi = (q * jnp.exp(gc - g_mid)).astype(BF16)
        ki = (k * jnp.exp(g_mid - gc)).astype(BF16)
        scores = lax.dot_general(qi, ki, (((1,), (1,)), ((), ())), preferred_element_type=F32)
        scores = jnp.where(tri > 0.5, scores, 0.0)
        intra = jnp.dot(scores.astype(BF16), v.astype(BF16), preferred_element_type=F32)
    else:
        row_id = lax.broadcasted_iota(jnp.int32, (chunk, 1), 0)

        def row(i, carry):
            sel = row_id == i
            qrow = jnp.sum(jnp.where(sel, q, 0.0), axis=0, keepdims=True)
            grow = jnp.sum(jnp.where(sel, gc, 0.0), axis=0, keepdims=True)
            decay = jnp.exp(jnp.minimum(grow - gc, 0.0))
            s = jnp.sum(qrow * k * decay, axis=-1, keepdims=True)
            s = jnp.where(row_id <= i, s, 0.0)
            intra_ref[pl.ds(i, 1), :] = jnp.sum(s * v, axis=0, keepdims=True)
            return carry

        lax.fori_loop(0, chunk, row, 0)
        intra = intra_ref[...]
    o_ref[rows, :] = _gla_gate_out(inter + intra, r_ref[rows, :], ng_ref[...]).astype(o_ref.dtype)
    kd = (k * jnp.exp(g_last - gc)).astype(BF16)
    return st * jnp.exp(g_last) + lax.dot_general(v.astype(BF16), kd, (((0,), (0,)), ((), ())),
                                                  preferred_element_type=F32)


def _gla_prompt_kernel(q_ref, k_ref, v_ref, g_ref, r_ref, ng_ref, tri_ref, o_ref, s_ref, st_ref, intra_ref,
                       *, chunk, n_chunks, dk):
    refs = (q_ref, k_ref, v_ref, g_ref, r_ref, ng_ref, o_ref)
    tri = tri_ref[...]
    chunk_decay = jnp.sum(g_ref[...].reshape(n_chunks, chunk, LANES), axis=1)
    factorised_ok = jnp.min(chunk_decay) >= -GLA_FACTORISED_MAX_DECAY

    @pl.when(factorised_ok)
    def _():
        st = jnp.zeros((LANES, LANES), F32)
        for c in range(n_chunks):
            st = _gla_chunk(refs, slice(c * chunk, (c + 1) * chunk), st, tri, exact=False)
        st_ref[...] = st

    @pl.when(jnp.logical_not(factorised_ok))
    def _():
        def body(c, st):
            rows = pl.ds(pl.multiple_of(c * chunk, chunk), chunk)
            return _gla_chunk(refs, rows, st, tri, exact=True, intra_ref=intra_ref)
        st_ref[...] = lax.fori_loop(0, n_chunks, body, jnp.zeros((LANES, LANES), F32))

    s_ref[...] = st_ref[...].T[:dk, :]


def _gla_prompt(gq, gk, gv, gg, gr, norm_g, *, batch, seq, heads, dk, chunk=128):
    m = gq.shape[0]
    n_chunks = seq // chunk
    tri = (lax.broadcasted_iota(jnp.int32, (chunk, chunk), 0)
           >= lax.broadcasted_iota(jnp.int32, (chunk, chunk), 1)).astype(F32)
    kern = functools.partial(_gla_prompt_kernel, chunk=chunk, n_chunks=n_chunks, dk=dk)
    blk = pl.BlockSpec((seq, LANES), lambda b, h: (b, h))
    return pl.pallas_call(
        kern,
        grid=(batch, heads),
        in_specs=[blk, blk, blk, blk, blk,
                  pl.BlockSpec((1, LANES), lambda b, h: (0, h)),
                  pl.BlockSpec((chunk, chunk), lambda b, h: (0, 0))],
        out_specs=[blk, pl.BlockSpec((None, None, dk, LANES), lambda b, h: (b, h, 0, 0))],
        out_shape=[jax.ShapeDtypeStruct((m, heads * LANES), BF16),
                   jax.ShapeDtypeStruct((batch, heads, dk, LANES), F32)],
        scratch_shapes=[pltpu.VMEM((LANES, LANES), F32), pltpu.VMEM((chunk, LANES), F32)],
        compiler_params=_params(("parallel", "parallel")),
        name="gla_prompt",
    )(gq, gk, gv, gg, gr, norm_g, tri)


def _mem_kv_kernel(x_ref, wk_ref, wv_ref, kf_ref, vf_ref, kb_ref, vb_ref):
    xb = x_ref[...].astype(BF16)
    k = jnp.dot(xb, wk_ref[...], preferred_element_type=F32)
    v = jnp.dot(xb, wv_ref[...], preferred_element_type=F32)
    kf_ref[...] = k
    vf_ref[...] = v
    kb_ref[...] = k.astype(BF16)
    vb_ref[...] = v.astype(BF16)


def _mem_kv(mem2d, wk, wv, *, tm=512):
    m, d = mem2d.shape
    w = wk.shape[1]
    row = lambda width: pl.BlockSpec((tm, width), lambda i: (i, 0))
    const = pl.BlockSpec((d, w), lambda i: (0, 0))
    return pl.pallas_call(
        _mem_kv_kernel,
        grid=(m // tm,),
        in_specs=[row(d), const, const],
        out_specs=[row(w)] * 4,
        out_shape=[jax.ShapeDtypeStruct((m, w), F32)] * 2 + [jax.ShapeDtypeStruct((m, w), BF16)] * 2,
        compiler_params=_params(("parallel",)),
        name="mem_kv",
    )(mem2d, wk, wv)


def _softmax_rows(s):
    s = s - jnp.max(s, axis=-1, keepdims=True)
    p = jnp.exp(s)
    return p / jnp.sum(p, axis=-1, keepdims=True)


def _wo_ln_kernel(*refs, alpha, sba_w, gvw, mem_heads, hd, fuse_mem):
    if fuse_mem:
        x_ref, a_ref, g_ref, mq_ref, mk_ref, mv_ref, wo_ref, lg_ref, lb_ref, o_ref = refs
    else:
        x_ref, a_ref, g_ref, om_ref, wo_ref, lg_ref, lb_ref, o_ref = refs
    acc = alpha * x_ref[...]
    acc = acc + jnp.dot(a_ref[...], wo_ref[0:sba_w, :], preferred_element_type=F32)
    acc = acc + jnp.dot(g_ref[...], wo_ref[sba_w:sba_w + gvw, :], preferred_element_type=F32)
    base = sba_w + gvw
    if fuse_mem:
        for h in range(mem_heads):
            lanes = slice(h * hd, (h + 1) * hd)
            s = lax.dot_general(mq_ref[:, lanes], mk_ref[:, lanes], (((1,), (1,)), ((), ())),
                                preferred_element_type=F32)
            p = _softmax_rows(s)
            om = jnp.dot(p.astype(BF16), mv_ref[:, lanes], preferred_element_type=F32)
            acc = acc + jnp.dot(om.astype(BF16), wo_ref[base + h * hd:base + (h + 1) * hd, :],
                                preferred_element_type=F32)
    else:
        acc = acc + jnp.dot(om_ref[...], wo_ref[base:, :], preferred_element_type=F32)
    o_ref[...] = _layer_norm(acc, lg_ref[...], lb_ref[...])


def _wo_ln(x2d, o_sba, og, mem_args, wo, ln_g, ln_b, *, tm, alpha, mem_heads, hd, rows_per_batch=None):
    m, d = x2d.shape
    sba_w, gvw = o_sba.shape[1], og.shape[1]
    fuse_mem = rows_per_batch is not None
    row = lambda w: pl.BlockSpec((tm, w), lambda i: (i, 0))
    const = lambda shape: pl.BlockSpec(shape, lambda i: (0, 0), pipeline_mode=pl.Buffered(1))
    if fuse_mem:
        mq, mk, mv = mem_args
        n_mem = mk.shape[0] // (m // rows_per_batch)
        tiles_per_batch = rows_per_batch // tm
        per_batch = pl.BlockSpec((n_mem, mk.shape[1]), lambda i: (i // tiles_per_batch, 0))
        mem_specs = [row(mq.shape[1]), per_batch, per_batch]
    else:
        mem_specs = [row(mem_args[0].shape[1])]
    kern = functools.partial(_wo_ln_kernel, alpha=alpha, sba_w=sba_w, gvw=gvw, mem_heads=mem_heads, hd=hd,
                             fuse_mem=fuse_mem)
    return pl.pallas_call(
        kern,
        grid=(m // tm,),
        in_specs=[row(d), row(sba_w), row(gvw)] + mem_specs + [const(wo.shape), const(ln_g.shape), const(ln_b.shape)],
        out_specs=row(d),
        out_shape=jax.ShapeDtypeStruct((m, d), F32),
        compiler_params=_params(("parallel",)),
        name="wo_ln1",
    )(x2d, o_sba, og, *mem_args, wo, ln_g, ln_b)


def _mlp_kernel(x_ref, wu_ref, wd_ref, lg_ref, lb_ref, o_ref, xb_ref, acc_ref, *, alpha):
    f = pl.program_id(1)

    @pl.when(f == 0)
    def _():
        xb_ref[...] = x_ref[...].astype(BF16)
        acc_ref[...] = jnp.zeros_like(acc_ref)

    h = jnp.dot(xb_ref[...], wu_ref[...], preferred_element_type=F32)
    u = jnp.square(jnp.maximum(h, 0.0))
    acc_ref[...] += jnp.dot(u.astype(BF16), wd_ref[...], preferred_element_type=F32)

    @pl.when(f == pl.num_programs(1) - 1)
    def _():
        o_ref[...] = _layer_norm(alpha * x_ref[...] + acc_ref[...], lg_ref[...], lb_ref[...])


def _mlp_ln(x1, wu, wd, ln_g, ln_b, *, tm, tf, alpha):
    m, d = x1.shape
    dff = wu.shape[1]
    kern = functools.partial(_mlp_kernel, alpha=alpha)
    return pl.pallas_call(
        kern,
        grid=(m // tm, dff // tf),
        in_specs=[pl.BlockSpec((tm, d), lambda i, f: (i, 0)),
                  pl.BlockSpec((d, tf), lambda i, f: (0, f)),
                  pl.BlockSpec((tf, d), lambda i, f: (f, 0)),
                  pl.BlockSpec((1, d), lambda i, f: (0, 0)),
                  pl.BlockSpec((1, d), lambda i, f: (0, 0))],
        out_specs=pl.BlockSpec((tm, d), lambda i, f: (i, 0)),
        out_shape=jax.ShapeDtypeStruct((m, d), F32),
        scratch_shapes=[pltpu.VMEM((tm, d), BF16), pltpu.VMEM((tm, d), F32)],
        compiler_params=_params(("parallel", "arbitrary")),
        name="mlp_ln2",
    )(x1, wu, wd, ln_g, ln_b)


def _sba_decode_kernel(pt_ref, bias_ref, q_ref, *refs, pages_per_step, heads):
    k_refs = refs[:pages_per_step]
    v_refs = refs[pages_per_step:2 * pages_per_step]
    u_ref = refs[2 * pages_per_step]
    o_ref = refs[2 * pages_per_step + 1]
    acc_ref, run_ref = refs[2 * pages_per_step + 2:]
    c = pl.program_id(1)

    @pl.when(c == 0)
    def _():
        acc_ref[...] = jnp.zeros_like(acc_ref)
        run_ref[...] = jnp.zeros_like(run_ref)

    groups = k_refs[0].shape[0] // LANES
    n_blocks = pages_per_step * groups
    q = q_ref[...].astype(F32)
    blocks = []
    for j in range(pages_per_step):
        r = lax.dot_general(q, k_refs[j][...], (((1,), (1,)), ((), ())), preferred_element_type=F32)
        blocks += [r[:, g * LANES:(g + 1) * LANES] for g in reversed(range(groups))]
    z = jnp.concatenate(blocks, axis=0)
    z = (z + jnp.concatenate([bias_ref[...]] * n_blocks, axis=0)) * LN2
    shape = z.shape
    own = ((lax.broadcasted_iota(jnp.int32, shape, 0) & (heads - 1))
           == (lax.broadcasted_iota(jnp.int32, shape, 1) & (heads - 1)))
    sp = _softplus(z)
    log_keep = jnp.where(own, -sp, 0.0)
    log_beta = z - sp
    hi = log_keep.astype(BF16)
    lo = (log_keep - hi.astype(F32)).astype(BF16)
    ext = (jnp.dot(hi, u_ref[...], preferred_element_type=F32)
           + jnp.dot(lo, u_ref[...], preferred_element_type=F32))
    incl, total = ext[:, :LANES], ext[:, LANES:]
    run = run_ref[...]
    offs = []
    for t in range(n_blocks):
        offs.append(run)
        run = run + total[t * heads:(t + 1) * heads]
    run_ref[...] = run
    wgt = jnp.where(own, jnp.exp(log_beta + (incl - log_keep) + jnp.concatenate(offs, axis=0)), 0.0)
    acc = acc_ref[...]
    for j in range(pages_per_step):
        first = j * groups
        wj = jnp.concatenate([wgt[(first + groups - 1 - g) * heads:(first + groups - g) * heads]
                              for g in range(groups)], axis=1)
        acc = acc + jnp.dot(wj, v_refs[j][...], preferred_element_type=F32)
    acc_ref[...] = acc

    @pl.when(c == pl.num_programs(1) - 1)
    def _():
        o_ref[...] = acc


def _sba_decode(q, cache_k, cache_v, layer, page_table, bias, *, pages_per_step=16):
    n_seq, n_pages = page_table.shape
    depth, n_phys, page, heads, hd = cache_k.shape
    assert heads & (heads - 1) == 0 and LANES % heads == 0 and (page * heads) % LANES == 0
    rows = page * heads
    n_steps = n_pages // pages_per_step
    ck = cache_k.reshape(depth * n_phys, rows, hd)
    cv = cache_v.reshape(depth * n_phys, rows, hd)
    tri = lax.broadcasted_iota(jnp.int32, (LANES, LANES), 0) >= lax.broadcasted_iota(jnp.int32, (LANES, LANES), 1)
    u = jnp.concatenate([tri.astype(BF16), jnp.ones((LANES, LANES), BF16)], axis=1)
    bias_tile = jnp.broadcast_to(bias.reshape(heads, 1), (heads, LANES)).astype(F32)

    def page_spec(j):
        def index(b, c, pt):
            logical = n_pages - 1 - (c * pages_per_step + j)
            return (layer * n_phys + pt[b * n_pages + logical], 0, 0)
        return pl.BlockSpec((None, rows, hd), index)

    kern = functools.partial(_sba_decode_kernel, pages_per_step=pages_per_step, heads=heads)
    grid_spec = pltpu.PrefetchScalarGridSpec(
        num_scalar_prefetch=1,
        grid=(n_seq, n_steps),
        in_specs=[pl.BlockSpec(bias_tile.shape, lambda b, c, pt: (0, 0)),
                  pl.BlockSpec((None, heads, hd), lambda b, c, pt: (b, 0, 0))]
                 + [page_spec(j) for j in range(pages_per_step)]
                 + [page_spec(j) for j in range(pages_per_step)]
                 + [pl.BlockSpec(u.shape, lambda b, c, pt: (0, 0))],
        out_specs=pl.BlockSpec((None, heads, hd), lambda b, c, pt: (b, 0, 0)),
        scratch_shapes=[pltpu.VMEM((heads, hd), F32), pltpu.VMEM((heads, LANES), F32)],
    )
    out = pl.pallas_call(
        kern,
        grid_spec=grid_spec,
        out_shape=jax.ShapeDtypeStruct((n_seq, heads, hd), F32),
        compiler_params=_params(("parallel", "arbitrary")),
        name="sba_decode",
    )(page_table.reshape(-1), bias_tile, q.reshape(n_seq, heads, hd),
      *([ck] * pages_per_step), *([cv] * pages_per_step), u)
    return out.reshape(n_seq, heads * hd)


def _gla_step_kernel(q_ref, k_ref, g_ref, v_ref, r_ref, ng_ref, s_ref, o_ref, sn_ref):
    s_new = jnp.exp(g_ref[...]) * s_ref[...].astype(F32) + k_ref[...] * v_ref[...]
    sn_ref[...] = s_new
    o = jnp.sum(q_ref[...] * s_new, axis=2, keepdims=True)
    o_ref[...] = _gla_gate_out(o, r_ref[...], ng_ref[...]).astype(o_ref.dtype)


def _gla_step(q_col, k_col, g_col, v_row, r_row, norm_g, state, layer, *, bb=8):
    depth, n_seq, heads, dk, dv = state.shape
    steps = n_seq // bb
    col = pl.BlockSpec((bb, heads, dk, 1), lambda i: (i, 0, 0, 0))
    rowv = pl.BlockSpec((bb, heads, 1, dv), lambda i: (i, 0, 0, 0))
    st = pl.BlockSpec((bb, heads, dk, dv), lambda i: (i, 0, 0, 0))
    st_in = pl.BlockSpec((bb, heads, dk, dv), lambda i: (layer * steps + i, 0, 0, 0))
    return pl.pallas_call(
        _gla_step_kernel,
        grid=(steps,),
        in_specs=[col, col, col, rowv, rowv, pl.BlockSpec((1, heads, 1, dv), lambda i: (0, 0, 0, 0)), st_in],
        out_specs=[rowv, st],
        out_shape=[jax.ShapeDtypeStruct((n_seq, heads, 1, dv), BF16),
                   jax.ShapeDtypeStruct((n_seq, heads, dk, dv), F32)],
        compiler_params=_params(("parallel",)),
        name="gla_step",
    )(q_col, k_col, g_col, v_row, r_row, norm_g, state.reshape(depth * n_seq, heads, dk, dv))


def _mem_decode_kernel(q_ref, k_ref, v_ref, o_ref, *, heads, hd):
    w = heads * hd
    head_of_lane = lax.broadcasted_iota(jnp.int32, (heads, w), 1) // hd
    own = head_of_lane == lax.broadcasted_iota(jnp.int32, (heads, w), 0)
    q_bd = jnp.where(own, jnp.broadcast_to(q_ref[...].astype(F32), (heads, w)), 0.0)
    s = lax.dot_general(q_bd, k_ref[...], (((1,), (1,)), ((), ())), preferred_element_type=F32)
    p = _softmax_rows(s)
    o = jnp.dot(p, v_ref[...], preferred_element_type=F32)
    o_ref[...] = jnp.sum(jnp.where(own, o, 0.0), axis=0, keepdims=True).astype(o_ref.dtype)


def _mem_decode(mq, mem_k, mem_v, layer):
    depth, n_seq, n_mem, heads, hd = mem_k.shape
    w = heads * hd
    kern = functools.partial(_mem_decode_kernel, heads=heads, hd=hd)
    vec = pl.BlockSpec((None, 1, w), lambda b: (b, 0, 0))
    kv = pl.BlockSpec((None, n_mem, w), lambda b: (layer * n_seq + b, 0, 0))
    out = pl.pallas_call(
        kern,
        grid=(n_seq,),
        in_specs=[vec, kv, kv],
        out_specs=vec,
        out_shape=jax.ShapeDtypeStruct((n_seq, 1, w), BF16),
        compiler_params=_params(("parallel",)),
        name="mem_decode",
    )(mq.reshape(n_seq, 1, w), mem_k.reshape(depth * n_seq, n_mem, w), mem_v.reshape(depth * n_seq, n_mem, w))
    return out.reshape(n_seq, w)


def kernel(x_prompt, x_sample, cache_sba_k, cache_sba_v, state_gla, cache_mem_k, cache_mem_v, page_table,
           mem_prompt, w_in, sba_bias, w_gate_up, b_gate, gla_norm_g, w_mem_k, w_mem_v, w_o, ln1_g, ln1_b,
           w_up, w_down, ln2_g, ln2_b):
    depth = w_in.shape[0]
    bp, tp, d = x_prompt.shape
    bs, ts, _ = x_sample.shape
    assert ts == 1, "the decode path handles one new token per sequence"
    _, n_phys, page, sba_h, sba_d = cache_sba_k.shape
    _, _, gla_h, gla_k, gla_v = state_gla.shape
    _, _, n_mem, mem_h, mem_d = cache_mem_k.shape
    rank = w_gate_up.shape[1]
    assert gla_v == LANES and gla_k <= LANES and sba_d == LANES and mem_d == LANES
    sba_w, gkw, gvw, mem_w = sba_h * sba_d, gla_h * gla_k, gla_h * gla_v, mem_h * mem_d
    gp_w = gla_h * LANES
    alpha = (2.0 * depth) ** 0.25
    dims = (sba_w, gkw, gvw, rank, mem_w, gla_h, gla_k)
    proj = functools.partial(_project, sba_w=sba_w, gp_w=gp_w, gvw=gvw, mem_w=mem_w,
                             sba_d=sba_d, gla_k=gla_k, mem_d=mem_d)

    yp = x_prompt.reshape(bp * tp, d)
    ys = x_sample.reshape(bs * ts, d)
    outs = [[] for _ in range(8)]
    for l in range(depth):
        w_packed, wg, bg = _pack_w_in(w_in[l], w_gate_up[l], b_gate[l], dims)
        wo = w_o[l].astype(BF16)
        wu = w_up[l].astype(BF16)
        wd = w_down[l].astype(BF16)
        ng = gla_norm_g[l].reshape(1, gvw)
        l1g, l1b = ln1_g[l].reshape(1, d), ln1_b[l].reshape(1, d)
        l2g, l2b = ln2_g[l].reshape(1, d), ln2_b[l].reshape(1, d)
        bias = sba_bias[l].astype(F32)

        q, kf, vf, kb, vb, gq, gk, gv, gr, gg, mq = proj(yp, w_packed, wg, bg, tm=256)
        o_sba = _sba_prompt(q, kb, vb, bias * LOG2E, batch=bp, seq=tp, heads=sba_h, hd=sba_d)
        og, s_p = _gla_prompt(gq, gk, gv, gg, gr, ng, batch=bp, seq=tp, heads=gla_h, dk=gla_k)
        mkf, mvf, mkb, mvb = _mem_kv(mem_prompt.reshape(bp * n_mem, d), w_mem_k[l].astype(BF16),
                                     w_mem_v[l].astype(BF16))
        x1 = _wo_ln(yp, o_sba, og, (mq, mkb, mvb), wo, l1g, l1b, tm=512, alpha=alpha, mem_heads=mem_h, hd=mem_d,
                    rows_per_batch=tp)
        yp_next = _mlp_ln(x1, wu, wd, l2g, l2b, tm=512, tf=1024, alpha=alpha)
        outs[0].append(kf.reshape(bp, tp, sba_h, sba_d))
        outs[1].append(vf.reshape(bp, tp, sba_h, sba_d))
        outs[2].append(s_p)
        outs[3].append(mkf.reshape(bp, n_mem, mem_h, mem_d))
        outs[4].append(mvf.reshape(bp, n_mem, mem_h, mem_d))

        q2, kf2, vf2, _, _, gq2, gk2, gv2, gr2, gg2, mq2 = proj(ys, w_packed, wg, bg, tm=bs)
        o_sba2 = _sba_decode(q2, cache_sba_k, cache_sba_v, l, page_table, bias * LOG2E).astype(BF16)
        col = lambda a: a.reshape(bs, gla_h, LANES)[:, :, :gla_k].reshape(bs, gla_h, gla_k, 1)
        rowv = lambda a: a.reshape(bs, gla_h, 1, gla_v)
        og2, s_s = _gla_step(col(gq2), col(gk2), col(gg2), rowv(gv2), rowv(gr2), ng.reshape(1, gla_h, 1, gla_v),
                             state_gla, l)
        o_mem2 = _mem_decode(mq2, cache_mem_k, cache_mem_v, l)
        x1s = _wo_ln(ys, o_sba2, og2.reshape(bs, gvw), (o_mem2,), wo, l1g, l1b, tm=bs, alpha=alpha,
                     mem_heads=mem_h, hd=mem_d)
        ys_next = _mlp_ln(x1s, wu, wd, l2g, l2b, tm=bs, tf=1024, alpha=alpha)
        outs[5].append(kf2.reshape(bs, ts, sba_h, sba_d))
        outs[6].append(vf2.reshape(bs, ts, sba_h, sba_d))
        outs[7].append(s_s)

        yp, ys = yp_next, ys_next

    return (yp.reshape(bp, tp, d), ys.reshape(bs, ts, d), *[jnp.stack(o) for o in outs])
```

```python
import functools
import math

import jax
import jax.numpy as jnp
from jax import lax
from jax.experimental import pallas as pl
from jax.experimental.pallas import tpu as pltpu

F32 = jnp.float32
BF16 = jnp.bfloat16

LANES = 128
LN_EPS = 1e-5
NORM_EPS = 1e-6
GLA_TAU = 16.0
LOG2E = math.log2(math.e)
LN2 = math.log(2.0)
VMEM_LIMIT_BYTES = 56 * 1024 * 1024


def _params(sem):
    return pltpu.CompilerParams(dimension_semantics=sem, vmem_limit_bytes=VMEM_LIMIT_BYTES)


def _softplus(z):
    return jnp.maximum(z, 0.0) + jnp.log1p(jnp.exp(-jnp.abs(z)))


def _layer_norm(v, g, b):
    mu = jnp.mean(v, axis=-1, keepdims=True)
    vc = v - mu
    var = jnp.mean(vc * vc, axis=-1, keepdims=True)
    return vc * lax.rsqrt(var + LN_EPS) * g + b


def _pack_w_in(w_in, w_gate_up, b_gate, dims):
    sba_w, gkw, gvw, rank, mem_w, gh, gk = dims
    o = [0]
    for s in (sba_w, sba_w, sba_w, gkw, gkw, gvw, gvw, rank, mem_w):
        o.append(o[-1] + s)
    d = w_in.shape[0]

    def pad_heads(w):
        w = w.reshape(d, gh, gk)
        return jnp.pad(w, ((0, 0), (0, 0), (0, LANES - gk))).reshape(d, gh * LANES)

    cols = [w_in[:, o[0]:o[3]], pad_heads(w_in[:, o[3]:o[4]]), pad_heads(w_in[:, o[4]:o[5]]),
            w_in[:, o[5]:o[7]], w_in[:, o[8]:o[9]],
            jnp.pad(w_in[:, o[7]:o[8]], ((0, 0), (0, LANES - rank)))]
    w_packed = jnp.concatenate(cols, axis=1).astype(BF16)
    wg = jnp.pad(w_gate_up.reshape(rank, gh, gk), ((0, LANES - rank), (0, 0), (0, LANES - gk)))
    wg = wg.reshape(LANES, gh * LANES).astype(BF16)
    bg = jnp.pad(b_gate.reshape(1, gh, gk), ((0, 0), (0, 0), (0, LANES - gk))).reshape(1, gh * LANES)
    return w_packed, wg, bg.astype(F32)


def _proj_kernel(x_ref, w_ref, wg_ref, bg_ref,
                 q_ref, kf_ref, vf_ref, kb_ref, vb_ref, gq_ref, gk_ref, gv_ref, gr_ref, gg_ref, mq_ref,
                 *, sba_w, gp_w, gvw, mem_w, sba_scale, gla_scale, mem_scale):
    xb = x_ref[...].astype(BF16)

    def mm(lo, width):
        return jnp.dot(xb, w_ref[:, lo:lo + width], preferred_element_type=F32)

    c = 0
    q_ref[...] = (mm(c, sba_w) * sba_scale).astype(BF16); c += sba_w
    k = mm(c, sba_w); c += sba_w
    kf_ref[...] = k
    kb_ref[...] = k.astype(BF16)
    v = mm(c, sba_w); c += sba_w
    vf_ref[...] = v
    vb_ref[...] = v.astype(BF16)
    gq_ref[...] = mm(c, gp_w) * gla_scale; c += gp_w
    gk_ref[...] = mm(c, gp_w); c += gp_w
    gv_ref[...] = mm(c, gvw); c += gvw
    gr_ref[...] = mm(c, gvw); c += gvw
    mq_ref[...] = (mm(c, mem_w) * mem_scale).astype(BF16); c += mem_w
    glow = mm(c, LANES)
    pre = jnp.dot(glow.astype(BF16), wg_ref[...], preferred_element_type=F32) + bg_ref[...]
    gg_ref[...] = -_softplus(-pre) * (1.0 / GLA_TAU)


def _project(x2d, w_packed, wg, bg, *, tm, sba_w, gp_w, gvw, mem_w, sba_d, gla_k, mem_d):
    m, d = x2d.shape
    npk = w_packed.shape[1]
    kern = functools.partial(_proj_kernel, sba_w=sba_w, gp_w=gp_w, gvw=gvw, mem_w=mem_w,
                             sba_scale=sba_d ** -0.5 * LOG2E, gla_scale=gla_k ** -0.5, mem_scale=mem_d ** -0.5)
    row = lambda w: pl.BlockSpec((tm, w), lambda i: (i, 0))
    const = lambda shape: pl.BlockSpec(shape, lambda i: (0, 0), pipeline_mode=pl.Buffered(1))
    out_shapes = [(sba_w, BF16), (sba_w, F32), (sba_w, F32), (sba_w, BF16), (sba_w, BF16),
                  (gp_w, F32), (gp_w, F32), (gvw, F32), (gvw, F32), (gp_w, F32), (mem_w, BF16)]
    return pl.pallas_call(
        kern,
        grid=(m // tm,),
        in_specs=[row(d), const((d, npk)), const(wg.shape), const(bg.shape)],
        out_specs=[row(w) for w, _ in out_shapes],
        out_shape=[jax.ShapeDtypeStruct((m, w), dt) for w, dt in out_shapes],
        compiler_params=_params(("parallel",)),
        name="in_proj",
    )(x2d, w_packed, wg, bg)


def _sba_tile(q_ref, k_ref, v_ref, u_ref, acc_ref, run_ref, bias, h, start, *, tq, hd, diag):
    lanes = slice(h * hd, (h + 1) * hd)
    k = k_ref[pl.ds(start, tq), lanes]
    v = v_ref[pl.ds(start, tq), lanes]
    z = lax.dot_general(q_ref[:, lanes], k, (((1,), (1,)), ((), ())), preferred_element_type=F32) + bias
    sp = jnp.maximum(z, 0.0) + jnp.log2(1.0 + jnp.exp2(-jnp.abs(z)))
    log_beta = z - sp
    if diag:
        keep = lax.broadcasted_iota(jnp.int32, (tq, tq), 1) < lax.broadcasted_iota(jnp.int32, (tq, tq), 0)
        sp = jnp.where(keep, sp, 0.0)
        log_beta = jnp.where(keep, log_beta, -1e30)
    ext = jnp.dot(sp.astype(BF16), u_ref[...], preferred_element_type=F32)
    rest, total = ext[:, :tq], ext[:, tq:]
    if diag:
        w = jnp.exp2(log_beta - rest)
        acc_ref[:, lanes] = jnp.dot(w.astype(BF16), v, preferred_element_type=F32)
        run_ref[h] = total
    else:
        run = run_ref[h]
        w = jnp.exp2(log_beta - rest - jnp.concatenate([run] * (tq // LANES), axis=1))
        acc_ref[:, lanes] += jnp.dot(w.astype(BF16), v, preferred_element_type=F32)
        run_ref[h] = run + total


def _sba_prompt_kernel(bias_ref, q_ref, k_ref, v_ref, u_ref, o_ref, acc_ref, run_ref, *, tq, hd, heads_per_step):
    i = pl.program_id(2)
    hg = pl.program_id(1)
    tile = functools.partial(_sba_tile, q_ref, k_ref, v_ref, u_ref, acc_ref, run_ref, tq=tq, hd=hd)
    biases = [bias_ref[hg * heads_per_step + h] for h in range(heads_per_step)]

    for h in range(heads_per_step):
        tile(biases[h], h, pl.multiple_of(i * tq, tq), diag=True)

    def body(jj, carry):
        start = pl.multiple_of((i - 1 - jj) * tq, tq)
        for h in range(heads_per_step):
            tile(biases[h], h, start, diag=False)
        return carry

    lax.fori_loop(0, i, body, 0)
    o_ref[...] = acc_ref[...].astype(o_ref.dtype)


def _sba_prompt(q, k, v, bias, *, batch, seq, heads, hd, tq=256, heads_per_step=8):
    m = q.shape[0]
    nq = seq // tq
    w = heads_per_step * hd
    strict = lax.broadcasted_iota(jnp.int32, (tq, tq), 0) > lax.broadcasted_iota(jnp.int32, (tq, tq), 1)
    u = jnp.concatenate([strict.astype(BF16), jnp.ones((tq, LANES), BF16)], axis=1)
    kern = functools.partial(_sba_prompt_kernel, tq=tq, hd=hd, heads_per_step=heads_per_step)
    return pl.pallas_call(
        kern,
        grid=(batch, heads // heads_per_step, nq),
        in_specs=[pl.BlockSpec(memory_space=pltpu.SMEM),
                  pl.BlockSpec((tq, w), lambda b, h, i: (b * nq + i, h)),
                  pl.BlockSpec((seq, w), lambda b, h, i: (b, h)),
                  pl.BlockSpec((seq, w), lambda b, h, i: (b, h)),
                  pl.BlockSpec(u.shape, lambda b, h, i: (0, 0))],
        out_specs=pl.BlockSpec((tq, w), lambda b, h, i: (b * nq + i, h)),
        out_shape=jax.ShapeDtypeStruct((m, heads * hd), BF16),
        scratch_shapes=[pltpu.VMEM((tq, w), F32), pltpu.VMEM((heads_per_step, tq, LANES), F32)],
        compiler_params=_params(("parallel", "parallel", "arbitrary")),
        name="sba_prompt",
    )(bias, q, k, v, u)


def _gla_gate_out(o, r, g):
    og = o * lax.rsqrt(jnp.mean(o * o, axis=-1, keepdims=True) + NORM_EPS)
    return og * g * (r * jax.nn.sigmoid(r))


GLA_FACTORISED_MAX_DECAY = 80.0


def _gla_chunk(refs, rows, st, tri, *, exact, intra_ref=None):
    q_ref, k_ref, v_ref, g_ref, r_ref, ng_ref, o_ref = refs
    chunk = tri.shape[0]
    q = q_ref[rows, :]
    k = k_ref[rows, :]
    v = v_ref[rows, :]
    gc = jnp.dot(tri, g_ref[rows, :], preferred_element_type=F32, precision=lax.Precision.HIGHEST)
    g_last = gc[chunk - 1:chunk, :]
    inter = lax.dot_general((q * jnp.exp(gc)).astype(BF16), st.astype(BF16),
                            (((1,), (1,)), ((), ())), preferred_element_type=F32)
    if not exact:
        g_mid = gc[chunk // 2 - 1:chunk // 2, :]
        qi = (q * jnp.exp(gc - g_mid)).astype(BF16)
        ki = (k * jnp.exp(g_mid - gc)).astype(BF16)
        scores = lax.dot_general(qi, ki, (((1,), (1,)), ((), ())), preferred_element_type=F32)
        scores = jnp.where(tri > 0.5, scores, 0.0)
        intra = jnp.dot(scores.astype(BF16), v.astype(BF16), preferred_element_type=F32)
    else:
        row_id = lax.broadcasted_iota(jnp.int32, (chunk, 1), 0)

        def row(i, carry):
            sel = row_id == i
            qrow = jnp.sum(jnp.where(sel, q, 0.0), axis=0, keepdims=True)
            grow = jnp.sum(jnp.where(sel, gc, 0.0), axis=0, keepdims=True)
            decay = jnp.exp(jnp.minimum(grow - gc, 0.0))
            s = jnp.sum(qrow * k * decay, axis=-1, keepdims=True)
            s = jnp.where(row_id <= i, s, 0.0)
            intra_ref[pl.ds(i, 1), :] = jnp.sum(s * v, axis=0, keepdims=True)
            return carry

        lax.fori_loop(0, chunk, row, 0)
        intra = intra_ref[...]
    o_ref[rows, :] = _gla_gate_out(inter + intra, r_ref[rows, :], ng_ref[...]).astype(o_ref.dtype)
    kd = (k * jnp.exp(g_last - gc)).astype(BF16)
    return st * jnp.exp(g_last) + lax.dot_general(v.astype(BF16), kd, (((0,), (0,)), ((), ())),
                                                  preferred_element_type=F32)


def _gla_prompt_kernel(q_ref, k_ref, v_ref, g_ref, r_ref, ng_ref, tri_ref, o_ref, s_ref, st_ref, intra_ref,
                       *, chunk, n_chunks, dk):
    refs = (q_ref, k_ref, v_ref, g_ref, r_ref, ng_ref, o_ref)
    tri = tri_ref[...]
    chunk_decay = jnp.sum(g_ref[...].reshape(n_chunks, chunk, LANES), axis=1)
    factorised_ok = jnp.min(chunk_decay) >= -GLA_FACTORISED_MAX_DECAY

    @pl.when(factorised_ok)
    def _():
        st = jnp.zeros((LANES, LANES), F32)
        for c in range(n_chunks):
            st = _gla_chunk(refs, slice(c * chunk, (c + 1) * chunk), st, tri, exact=False)
        st_ref[...] = st

    @pl.when(jnp.logical_not(factorised_ok))
    def _():
        def body(c, st):
            rows = pl.ds(pl.multiple_of(c * chunk, chunk), chunk)
            return _gla_chunk(refs, rows, st, tri, exact=True, intra_ref=intra_ref)
        st_ref[...] = lax.fori_loop(0, n_chunks, body, jnp.zeros((LANES, LANES), F32))

    s_ref[...] = st_ref[...].T[:dk, :]


def _gla_prompt(gq, gk, gv, gg, gr, norm_g, *, batch, seq, heads, dk, chunk=128):
    m = gq.shape[0]
    n_chunks = seq // chunk
    tri = (lax.broadcasted_iota(jnp.int32, (chunk, chunk), 0)
           >= lax.broadcasted_iota(jnp.int32, (chunk, chunk), 1)).astype(F32)
    kern = functools.partial(_gla_prompt_kernel, chunk=chunk, n_chunks=n_chunks, dk=dk)
    blk = pl.BlockSpec((seq, LANES), lambda b, h: (b, h))
    return pl.pallas_call(
        kern,
        grid=(batch, heads),
        in_specs=[blk, blk, blk, blk, blk,
                  pl.BlockSpec((1, LANES), lambda b, h: (0, h)),
                  pl.BlockSpec((chunk, chunk), lambda b, h: (0, 0))],
        out_specs=[blk, pl.BlockSpec((None, None, dk, LANES), lambda b, h: (b, h, 0, 0))],
        out_shape=[jax.ShapeDtypeStruct((m, heads * LANES), BF16),
                   jax.ShapeDtypeStruct((batch, heads, dk, LANES), F32)],
        scratch_shapes=[pltpu.VMEM((LANES, LANES), F32), pltpu.VMEM((chunk, LANES), F32)],
        compiler_params=_params(("parallel", "parallel")),
        name="gla_prompt",
    )(gq, gk, gv, gg, gr, norm_g, tri)


def _mem_kv_kernel(x_ref, wk_ref, wv_ref, kf_ref, vf_ref, kb_ref, vb_ref):
    xb = x_ref[...].astype(BF16)
    k = jnp.dot(xb, wk_ref[...], preferred_element_type=F32)
    v = jnp.dot(xb, wv_ref[...], preferred_element_type=F32)
    kf_ref[...] = k
    vf_ref[...] = v
    kb_ref[...] = k.astype(BF16)
    vb_ref[...] = v.astype(BF16)


def _mem_kv(mem2d, wk, wv, *, tm=512):
    m, d = mem2d.shape
    w = wk.shape[1]
    row = lambda width: pl.BlockSpec((tm, width), lambda i: (i, 0))
    const = pl.BlockSpec((d, w), lambda i: (0, 0))
    return pl.pallas_call(
        _mem_kv_kernel,
        grid=(m // tm,),
        in_specs=[row(d), const, const],
        out_specs=[row(w)] * 4,
        out_shape=[jax.ShapeDtypeStruct((m, w), F32)] * 2 + [jax.ShapeDtypeStruct((m, w), BF16)] * 2,
        compiler_params=_params(("parallel",)),
        name="mem_kv",
    )(mem2d, wk, wv)


def _softmax_rows(s):
    s = s - jnp.max(s, axis=-1, keepdims=True)
    p = jnp.exp(s)
    return p / jnp.sum(p, axis=-1, keepdims=True)


def _wo_ln_kernel(*refs, alpha, sba_w, gvw, mem_heads, hd, fuse_mem):
    if fuse_mem:
        x_ref, a_ref, g_ref, mq_ref, mk_ref, mv_ref, wo_ref, lg_ref, lb_ref, o_ref = refs
    else:
        x_ref, a_ref, g_ref, om_ref, wo_ref, lg_ref, lb_ref, o_ref = refs
    acc = alpha * x_ref[...]
    acc = acc + jnp.dot(a_ref[...], wo_ref[0:sba_w, :], preferred_element_type=F32)
    acc = acc + jnp.dot(g_ref[...], wo_ref[sba_w:sba_w + gvw, :], preferred_element_type=F32)
    base = sba_w + gvw
    if fuse_mem:
        for h in range(mem_heads):
            lanes = slice(h * hd, (h + 1) * hd)
            s = lax.dot_general(mq_ref[:, lanes], mk_ref[:, lanes], (((1,), (1,)), ((), ())),
                                preferred_element_type=F32)
            p = _softmax_rows(s)
            om = jnp.dot(p.astype(BF16), mv_ref[:, lanes], preferred_element_type=F32)
            acc = acc + jnp.dot(om.astype(BF16), wo_ref[base + h * hd:base + (h + 1) * hd, :],
                                preferred_element_type=F32)
    else:
        acc = acc + jnp.dot(om_ref[...], wo_ref[base:, :], preferred_element_type=F32)
    o_ref[...] = _layer_norm(acc, lg_ref[...], lb_ref[...])


def _wo_ln(x2d, o_sba, og, mem_args, wo, ln_g, ln_b, *, tm, alpha, mem_heads, hd, rows_per_batch=None):
    m, d = x2d.shape
    sba_w, gvw = o_sba.shape[1], og.shape[1]
    fuse_mem = rows_per_batch is not None
    row = lambda w: pl.BlockSpec((tm, w), lambda i: (i, 0))
    const = lambda shape: pl.BlockSpec(shape, lambda i: (0, 0), pipeline_mode=pl.Buffered(1))
    if fuse_mem:
        mq, mk, mv = mem_args
        n_mem = mk.shape[0] // (m // rows_per_batch)
        tiles_per_batch = rows_per_batch // tm
        per_batch = pl.BlockSpec((n_mem, mk.shape[1]), lambda i: (i // tiles_per_batch, 0))
        mem_specs = [row(mq.shape[1]), per_batch, per_batch]
    else:
        mem_specs = [row(mem_args[0].shape[1])]
    kern = functools.partial(_wo_ln_kernel, alpha=alpha, sba_w=sba_w, gvw=gvw, mem_heads=mem_heads, hd=hd,
                             fuse_mem=fuse_mem)
    return pl.pallas_call(
        kern,
        grid=(m // tm,),
        in_specs=[row(d), row(sba_w), row(gvw)] + mem_specs + [const(wo.shape), const(ln_g.shape), const(ln_b.shape)],
        out_specs=row(d),
        out_shape=jax.ShapeDtypeStruct((m, d), F32),
        compiler_params=_params(("parallel",)),
        name="wo_ln1",
    )(x2d, o_sba, og, *mem_args, wo, ln_g, ln_b)


def _mlp_init(x_ref, xb_ref, acc_ref):
    xb_ref[...] = x_ref[...].astype(BF16)
    acc_ref[...] = jnp.zeros_like(acc_ref)


def _mlp_accumulate(xb_ref, wu_ref, wd_ref, acc_ref):
    h = jnp.dot(xb_ref[...], wu_ref[...], preferred_element_type=F32)
    u = jnp.square(jnp.maximum(h, 0.0))
    acc_ref[...] += jnp.dot(u.astype(BF16), wd_ref[...], preferred_element_type=F32)


def _mlp_finish(x_ref, acc_ref, lg_ref, lb_ref, o_ref, alpha):
    o_ref[...] = _layer_norm(alpha * x_ref[...] + acc_ref[...], lg_ref[...], lb_ref[...])


def _mlp_kernel(x_ref, wu_ref, wd_ref, lg_ref, lb_ref, o_ref, xb_ref, acc_ref, *, alpha):
    f = pl.program_id(1)

    @pl.when(f == 0)
    def _():
        _mlp_init(x_ref, xb_ref, acc_ref)

    _mlp_accumulate(xb_ref, wu_ref, wd_ref, acc_ref)

    @pl.when(f == pl.num_programs(1) - 1)
    def _():
        _mlp_finish(x_ref, acc_ref, lg_ref, lb_ref, o_ref, alpha)


def _mlp_ln(x1, wu, wd, ln_g, ln_b, *, tm, tf, alpha):
    m, d = x1.shape
    dff = wu.shape[1]
    kern = functools.partial(_mlp_kernel, alpha=alpha)
    return pl.pallas_call(
        kern,
        grid=(m // tm, dff // tf),
        in_specs=[pl.BlockSpec((tm, d), lambda i, f: (i, 0)),
                  pl.BlockSpec((d, tf), lambda i, f: (0, f)),
                  pl.BlockSpec((tf, d), lambda i, f: (f, 0)),
                  pl.BlockSpec((1, d), lambda i, f: (0, 0)),
                  pl.BlockSpec((1, d), lambda i, f: (0, 0))],
        out_specs=pl.BlockSpec((tm, d), lambda i, f: (i, 0)),
        out_shape=jax.ShapeDtypeStruct((m, d), F32),
        scratch_shapes=[pltpu.VMEM((tm, d), BF16), pltpu.VMEM((tm, d), F32)],
        compiler_params=_params(("parallel", "arbitrary")),
        name="mlp_ln2",
    )(x1, wu, wd, ln_g, ln_b)


def _decode_init(acc_ref, run_ref):
    acc_ref[...] = jnp.zeros_like(acc_ref)
    run_ref[...] = jnp.zeros_like(run_ref)


def _decode_accumulate(bias_ref, q_ref, k_refs, v_refs, u_ref, acc_ref, run_ref, heads):
    pages_per_step = len(k_refs)
    groups = k_refs[0].shape[0] // LANES
    n_blocks = pages_per_step * groups
    q = q_ref[...].astype(F32)
    blocks = []
    for j in range(pages_per_step):
        r = lax.dot_general(q, k_refs[j][...], (((1,), (1,)), ((), ())), preferred_element_type=F32)
        blocks += [r[:, g * LANES:(g + 1) * LANES] for g in reversed(range(groups))]
    z = jnp.concatenate(blocks, axis=0)
    z = (z + jnp.concatenate([bias_ref[...]] * n_blocks, axis=0)) * LN2
    shape = z.shape
    own = ((lax.broadcasted_iota(jnp.int32, shape, 0) & (heads - 1))
           == (lax.broadcasted_iota(jnp.int32, shape, 1) & (heads - 1)))
    sp = _softplus(z)
    log_keep = jnp.where(own, -sp, 0.0)
    log_beta = z - sp
    hi = log_keep.astype(BF16)
    lo = (log_keep - hi.astype(F32)).astype(BF16)
    ext = (jnp.dot(hi, u_ref[...], preferred_element_type=F32)
           + jnp.dot(lo, u_ref[...], preferred_element_type=F32))
    incl, total = ext[:, :LANES], ext[:, LANES:]
    run = run_ref[...]
    offs = []
    for t in range(n_blocks):
        offs.append(run)
        run = run + total[t * heads:(t + 1) * heads]
    run_ref[...] = run
    wgt = jnp.where(own, jnp.exp(log_beta + (incl - log_keep) + jnp.concatenate(offs, axis=0)), 0.0)
    acc = acc_ref[...]
    for j in range(pages_per_step):
        first = j * groups
        wj = jnp.concatenate([wgt[(first + groups - 1 - g) * heads:(first + groups - g) * heads]
                              for g in range(groups)], axis=1)
        acc = acc + jnp.dot(wj, v_refs[j][...], preferred_element_type=F32)
    acc_ref[...] = acc


def _sba_decode_kernel(pt_ref, bias_ref, q_ref, *refs, pages_per_step, heads):
    k_refs = refs[:pages_per_step]
    v_refs = refs[pages_per_step:2 * pages_per_step]
    u_ref, o_ref, acc_ref, run_ref = refs[2 * pages_per_step:]
    c = pl.program_id(1)

    @pl.when(c == 0)
    def _():
        _decode_init(acc_ref, run_ref)

    _decode_accumulate(bias_ref, q_ref, k_refs, v_refs, u_ref, acc_ref, run_ref, heads)

    @pl.when(c == pl.num_programs(1) - 1)
    def _():
        o_ref[...] = acc_ref[...]


def _mlp_decode_kernel(pt_ref, x_ref, wu_ref, wd_ref, lg_ref, lb_ref, bias_ref, q_ref, *refs,
                       alpha, pages_per_step, heads):
    k_refs = refs[:pages_per_step]
    v_refs = refs[pages_per_step:2 * pages_per_step]
    u_ref, o_ref, osba_ref, xb_ref, acc_ref, dacc_ref, drun_ref = refs[2 * pages_per_step:]
    f = pl.program_id(1)

    @pl.when(f == 0)
    def _():
        _mlp_init(x_ref, xb_ref, acc_ref)
        _decode_init(dacc_ref, drun_ref)

    _mlp_accumulate(xb_ref, wu_ref, wd_ref, acc_ref)
    _decode_accumulate(bias_ref, q_ref, k_refs, v_refs, u_ref, dacc_ref, drun_ref, heads)

    @pl.when(f == pl.num_programs(1) - 1)
    def _():
        _mlp_finish(x_ref, acc_ref, lg_ref, lb_ref, o_ref, alpha)
        osba_ref[...] = dacc_ref[...]


def _decode_operands(q, cache_k, cache_v, layer, page_table, bias, pages_per_step):
    n_seq, n_pages = page_table.shape
    depth, n_phys, page, heads, hd = cache_k.shape
    assert heads & (heads - 1) == 0 and LANES % heads == 0 and (page * heads) % LANES == 0
    rows = page * heads
    ck = cache_k.reshape(depth * n_phys, rows, hd)
    cv = cache_v.reshape(depth * n_phys, rows, hd)
    tri = lax.broadcasted_iota(jnp.int32, (LANES, LANES), 0) >= lax.broadcasted_iota(jnp.int32, (LANES, LANES), 1)
    u = jnp.concatenate([tri.astype(BF16), jnp.ones((LANES, LANES), BF16)], axis=1)
    bias_tile = jnp.broadcast_to(bias.reshape(heads, 1), (heads, LANES)).astype(F32)

    def page_spec(j):
        def index(b, c, pt):
            logical = n_pages - 1 - (c * pages_per_step + j)
            return (layer * n_phys + pt[b * n_pages + logical], 0, 0)
        return pl.BlockSpec((None, rows, hd), index)

    seq_spec = pl.BlockSpec((None, heads, hd), lambda b, c, pt: (b, 0, 0))
    head_specs = [pl.BlockSpec(bias_tile.shape, lambda b, c, pt: (0, 0)), seq_spec]
    page_specs = [page_spec(j) for j in range(pages_per_step)] * 2 + [pl.BlockSpec(u.shape, lambda b, c, pt: (0, 0))]
    head_args = (bias_tile, q.reshape(n_seq, heads, hd))
    page_args = (*([ck] * pages_per_step), *([cv] * pages_per_step), u)
    scratch = [pltpu.VMEM((heads, hd), F32), pltpu.VMEM((heads, LANES), F32)]
    return head_specs, page_specs, head_args, page_args, seq_spec, scratch


def _sba_decode(q, cache_k, cache_v, layer, page_table, bias, *, pages_per_step=16):
    n_seq, n_pages = page_table.shape
    heads, hd = cache_k.shape[3:]
    head_specs, page_specs, head_args, page_args, seq_spec, scratch = _decode_operands(
        q, cache_k, cache_v, layer, page_table, bias, pages_per_step)
    kern = functools.partial(_sba_decode_kernel, pages_per_step=pages_per_step, heads=heads)
    grid_spec = pltpu.PrefetchScalarGridSpec(
        num_scalar_prefetch=1,
        grid=(n_seq, n_pages // pages_per_step),
        in_specs=head_specs + page_specs,
        out_specs=seq_spec,
        scratch_shapes=scratch,
    )
    out = pl.pallas_call(
        kern,
        grid_spec=grid_spec,
        out_shape=jax.ShapeDtypeStruct((n_seq, heads, hd), F32),
        compiler_params=_params(("parallel", "arbitrary")),
        name="sba_decode",
    )(page_table.reshape(-1), *head_args, *page_args)
    return out.reshape(n_seq, heads * hd)


SHARED_GRID_MAX_TM = 512
SHARED_GRID_TF = 512


def _shared_grid_tiles(m, dff, n_seq, n_pages):
    if m % n_seq or dff % SHARED_GRID_TF:
        return None, None
    tm, col_steps = m // n_seq, dff // SHARED_GRID_TF
    if tm % 8 or tm > SHARED_GRID_MAX_TM or n_pages % col_steps:
        return None, None
    return tm, SHARED_GRID_TF


def _mlp_ln_with_decode(x1, wu, wd, ln_g, ln_b, q, cache_k, cache_v, layer, page_table, bias, *, tm, tf, alpha):
    m, d = x1.shape
    dff = wu.shape[1]
    n_seq, n_pages = page_table.shape
    heads, hd = cache_k.shape[3:]
    grid = (m // tm, dff // tf)
    assert grid[0] == n_seq and n_pages % grid[1] == 0
    pages_per_step = n_pages // grid[1]
    head_specs, page_specs, head_args, page_args, seq_spec, scratch = _decode_operands(
        q, cache_k, cache_v, layer, page_table, bias, pages_per_step)
    kern = functools.partial(_mlp_decode_kernel, alpha=alpha, pages_per_step=pages_per_step, heads=heads)
    grid_spec = pltpu.PrefetchScalarGridSpec(
        num_scalar_prefetch=1,
        grid=grid,
        in_specs=[pl.BlockSpec((tm, d), lambda i, f, pt: (i, 0)),
                  pl.BlockSpec((d, tf), lambda i, f, pt: (0, f)),
                  pl.BlockSpec((tf, d), lambda i, f, pt: (f, 0)),
                  pl.BlockSpec((1, d), lambda i, f, pt: (0, 0)),
                  pl.BlockSpec((1, d), lambda i, f, pt: (0, 0))] + head_specs + page_specs,
        out_specs=[pl.BlockSpec((tm, d), lambda i, f, pt: (i, 0)), seq_spec],
        scratch_shapes=[pltpu.VMEM((tm, d), BF16), pltpu.VMEM((tm, d), F32)] + scratch,
    )
    y, o_sba = pl.pallas_call(
        kern,
        grid_spec=grid_spec,
        out_shape=[jax.ShapeDtypeStruct((m, d), F32), jax.ShapeDtypeStruct((n_seq, heads, hd), F32)],
        compiler_params=_params(("parallel", "arbitrary")),
        name="mlp_ln2_sba_decode",
    )(page_table.reshape(-1), x1, wu, wd, ln_g, ln_b, *head_args, *page_args)
    return y, o_sba.reshape(n_seq, heads * hd)


def _gla_step_kernel(q_ref, k_ref, g_ref, v_ref, r_ref, ng_ref, s_ref, o_ref, sn_ref):
    s_new = jnp.exp(g_ref[...]) * s_ref[...].astype(F32) + k_ref[...] * v_ref[...]
    sn_ref[...] = s_new
    o = jnp.sum(q_ref[...] * s_new, axis=2, keepdims=True)
    o_ref[...] = _gla_gate_out(o, r_ref[...], ng_ref[...]).astype(o_ref.dtype)


def _gla_step(q_col, k_col, g_col, v_row, r_row, norm_g, state, layer, *, bb=8):
    depth, n_seq, heads, dk, dv = state.shape
    steps = n_seq // bb
    col = pl.BlockSpec((bb, heads, dk, 1), lambda i: (i, 0, 0, 0))
    rowv = pl.BlockSpec((bb, heads, 1, dv), lambda i: (i, 0, 0, 0))
    st = pl.BlockSpec((bb, heads, dk, dv), lambda i: (i, 0, 0, 0))
    st_in = pl.BlockSpec((bb, heads, dk, dv), lambda i: (layer * steps + i, 0, 0, 0))
    return pl.pallas_call(
        _gla_step_kernel,
        grid=(steps,),
        in_specs=[col, col, col, rowv, rowv, pl.BlockSpec((1, heads, 1, dv), lambda i: (0, 0, 0, 0)), st_in],
        out_specs=[rowv, st],
        out_shape=[jax.ShapeDtypeStruct((n_seq, heads, 1, dv), BF16),
                   jax.ShapeDtypeStruct((n_seq, heads, dk, dv), F32)],
        compiler_params=_params(("parallel",)),
        name="gla_step",
    )(q_col, k_col, g_col, v_row, r_row, norm_g, state.reshape(depth * n_seq, heads, dk, dv))


def _mem_decode_kernel(q_ref, k_ref, v_ref, o_ref, *, heads, hd):
    w = heads * hd
    head_of_lane = lax.broadcasted_iota(jnp.int32, (heads, w), 1) // hd
    own = head_of_lane == lax.broadcasted_iota(jnp.int32, (heads, w), 0)
    q_bd = jnp.where(own, jnp.broadcast_to(q_ref[...].astype(F32), (heads, w)), 0.0)
    s = lax.dot_general(q_bd, k_ref[...], (((1,), (1,)), ((), ())), preferred_element_type=F32)
    p = _softmax_rows(s)
    o = jnp.dot(p, v_ref[...], preferred_element_type=F32)
    o_ref[...] = jnp.sum(jnp.where(own, o, 0.0), axis=0, keepdims=True).astype(o_ref.dtype)


def _mem_decode(mq, mem_k, mem_v, layer):
    depth, n_seq, n_mem, heads, hd = mem_k.shape
    w = heads * hd
    kern = functools.partial(_mem_decode_kernel, heads=heads, hd=hd)
    vec = pl.BlockSpec((None, 1, w), lambda b: (b, 0, 0))
    kv = pl.BlockSpec((None, n_mem, w), lambda b: (layer * n_seq + b, 0, 0))
    out = pl.pallas_call(
        kern,
        grid=(n_seq,),
        in_specs=[vec, kv, kv],
        out_specs=vec,
        out_shape=jax.ShapeDtypeStruct((n_seq, 1, w), BF16),
        compiler_params=_params(("parallel",)),
        name="mem_decode",
    )(mq.reshape(n_seq, 1, w), mem_k.reshape(depth * n_seq, n_mem, w), mem_v.reshape(depth * n_seq, n_mem, w))
    return out.reshape(n_seq, w)


def kernel(x_prompt, x_sample, cache_sba_k, cache_sba_v, state_gla, cache_mem_k, cache_mem_v, page_table,
           mem_prompt, w_in, sba_bias, w_gate_up, b_gate, gla_norm_g, w_mem_k, w_mem_v, w_o, ln1_g, ln1_b,
           w_up, w_down, ln2_g, ln2_b):
    depth = w_in.shape[0]
    bp, tp, d = x_prompt.shape
    bs, ts, _ = x_sample.shape
    assert ts == 1, "the decode path handles one new token per sequence"
    _, n_phys, page, sba_h, sba_d = cache_sba_k.shape
    _, _, gla_h, gla_k, gla_v = state_gla.shape
    _, _, n_mem, mem_h, mem_d = cache_mem_k.shape
    rank = w_gate_up.shape[1]
    assert gla_v == LANES and gla_k <= LANES and sba_d == LANES and mem_d == LANES
    sba_w, gkw, gvw, mem_w = sba_h * sba_d, gla_h * gla_k, gla_h * gla_v, mem_h * mem_d
    gp_w = gla_h * LANES
    alpha = (2.0 * depth) ** 0.25
    dims = (sba_w, gkw, gvw, rank, mem_w, gla_h, gla_k)
    proj = functools.partial(_project, sba_w=sba_w, gp_w=gp_w, gvw=gvw, mem_w=mem_w,
                             sba_d=sba_d, gla_k=gla_k, mem_d=mem_d)

    yp = x_prompt.reshape(bp * tp, d)
    ys = x_sample.reshape(bs * ts, d)
    outs = [[] for _ in range(8)]
    for l in range(depth):
        w_packed, wg, bg = _pack_w_in(w_in[l], w_gate_up[l], b_gate[l], dims)
        wo = w_o[l].astype(BF16)
        wu = w_up[l].astype(BF16)
        wd = w_down[l].astype(BF16)
        ng = gla_norm_g[l].reshape(1, gvw)
        l1g, l1b = ln1_g[l].reshape(1, d), ln1_b[l].reshape(1, d)
        l2g, l2b = ln2_g[l].reshape(1, d), ln2_b[l].reshape(1, d)
        bias = sba_bias[l].astype(F32)

        q, kf, vf, kb, vb, gq, gk, gv, gr, gg, mq = proj(yp, w_packed, wg, bg, tm=256)
        o_sba = _sba_prompt(q, kb, vb, bias * LOG2E, batch=bp, seq=tp, heads=sba_h, hd=sba_d)
        og, s_p = _gla_prompt(gq, gk, gv, gg, gr, ng, batch=bp, seq=tp, heads=gla_h, dk=gla_k)
        mkf, mvf, mkb, mvb = _mem_kv(mem_prompt.reshape(bp * n_mem, d), w_mem_k[l].astype(BF16),
                                     w_mem_v[l].astype(BF16))
        x1 = _wo_ln(yp, o_sba, og, (mq, mkb, mvb), wo, l1g, l1b, tm=512, alpha=alpha, mem_heads=mem_h, hd=mem_d,
                    rows_per_batch=tp)
        outs[0].append(kf.reshape(bp, tp, sba_h, sba_d))
        outs[1].append(vf.reshape(bp, tp, sba_h, sba_d))
        outs[2].append(s_p)
        outs[3].append(mkf.reshape(bp, n_mem, mem_h, mem_d))
        outs[4].append(mvf.reshape(bp, n_mem, mem_h, mem_d))

        q2, kf2, vf2, _, _, gq2, gk2, gv2, gr2, gg2, mq2 = proj(ys, w_packed, wg, bg, tm=bs)
        decode_args = (q2, cache_sba_k, cache_sba_v, l, page_table, bias * LOG2E)
        tm_shared, tf_shared = _shared_grid_tiles(bp * tp, wu.shape[1], *page_table.shape)
        if tm_shared is not None:
            yp_next, o_sba2 = _mlp_ln_with_decode(x1, wu, wd, l2g, l2b, *decode_args, tm=tm_shared, tf=tf_shared,
                                                  alpha=alpha)
        else:
            yp_next = _mlp_ln(x1, wu, wd, l2g, l2b, tm=512, tf=1024, alpha=alpha)
            o_sba2 = _sba_decode(*decode_args)
        o_sba2 = o_sba2.astype(BF16)
        col = lambda a: a.reshape(bs, gla_h, LANES)[:, :, :gla_k].reshape(bs, gla_h, gla_k, 1)
        rowv = lambda a: a.reshape(bs, gla_h, 1, gla_v)
        og2, s_s = _gla_step(col(gq2), col(gk2), col(gg2), rowv(gv2), rowv(gr2), ng.reshape(1, gla_h, 1, gla_v),
                             state_gla, l)
        o_mem2 = _mem_decode(mq2, cache_mem_k, cache_mem_v, l)
        x1s = _wo_ln(ys, o_sba2, og2.reshape(bs, gvw), (o_mem2,), wo, l1g, l1b, tm=bs, alpha=alpha,
                     mem_heads=mem_h, hd=mem_d)
        ys_next = _mlp_ln(x1s, wu, wd, l2g, l2b, tm=bs, tf=1024, alpha=alpha)
        outs[5].append(kf2.reshape(bs, ts, sba_h, sba_d))
        outs[6].append(vf2.reshape(bs, ts, sba_h, sba_d))
        outs[7].append(s_s)

        yp, ys = yp_next, ys_next

    return (yp.reshape(bp, tp, d), ys.reshape(bs, ts, d), *[jnp.stack(o) for o in outs])
```

```python
import functools
import math

import jax
import jax.numpy as jnp
from jax import lax
from jax.experimental import pallas as pl
from jax.experimental.pallas import tpu as pltpu

F32 = jnp.float32
BF16 = jnp.bfloat16

LANES = 128
LN_EPS = 1e-5
NORM_EPS = 1e-6
GLA_TAU = 16.0
LOG2E = math.log2(math.e)
LN2 = math.log(2.0)
VMEM_LIMIT_BYTES = 56 * 1024 * 1024


def _params(sem):
    return pltpu.CompilerParams(dimension_semantics=sem, vmem_limit_bytes=VMEM_LIMIT_BYTES)


def _softplus(z):
    return jnp.maximum(z, 0.0) + jnp.log1p(jnp.exp(-jnp.abs(z)))


def _layer_norm(v, g, b):
    mu = jnp.mean(v, axis=-1, keepdims=True)
    vc = v - mu
    var = jnp.mean(vc * vc, axis=-1, keepdims=True)
    return vc * lax.rsqrt(var + LN_EPS) * g + b


def _pack_w_in(w_in, w_gate_up, b_gate, dims):
    sba_w, gkw, gvw, rank, mem_w, gh, gk = dims
    o = [0]
    for s in (sba_w, sba_w, sba_w, gkw, gkw, gvw, gvw, rank, mem_w):
        o.append(o[-1] + s)
    d = w_in.shape[0]

    def pad_heads(w):
        w = w.reshape(d, gh, gk)
        return jnp.pad(w, ((0, 0), (0, 0), (0, LANES - gk))).reshape(d, gh * LANES)

    cols = [w_in[:, o[0]:o[3]], pad_heads(w_in[:, o[3]:o[4]]), pad_heads(w_in[:, o[4]:o[5]]),
            w_in[:, o[5]:o[7]], w_in[:, o[8]:o[9]],
            jnp.pad(w_in[:, o[7]:o[8]], ((0, 0), (0, LANES - rank)))]
    w_packed = jnp.concatenate(cols, axis=1).astype(BF16)
    wg = jnp.pad(w_gate_up.reshape(rank, gh, gk), ((0, LANES - rank), (0, 0), (0, LANES - gk)))
    wg = wg.reshape(LANES, gh * LANES).astype(BF16)
    bg = jnp.pad(b_gate.reshape(1, gh, gk), ((0, 0), (0, 0), (0, LANES - gk))).reshape(1, gh * LANES)
    return w_packed, wg, bg.astype(F32)


def _proj_kernel(x_ref, w_ref, wg_ref, bg_ref,
                 q_ref, kf_ref, vf_ref, kb_ref, vb_ref, gq_ref, gk_ref, gv_ref, gr_ref, gg_ref, mq_ref,
                 *, sba_w, gp_w, gvw, mem_w, sba_scale, gla_scale, mem_scale):
    xb = x_ref[...].astype(BF16)

    def mm(lo, width):
        return jnp.dot(xb, w_ref[:, lo:lo + width], preferred_element_type=F32)

    c = 0
    q_ref[...] = (mm(c, sba_w) * sba_scale).astype(BF16); c += sba_w
    k = mm(c, sba_w); c += sba_w
    kf_ref[...] = k
    kb_ref[...] = k.astype(BF16)
    v = mm(c, sba_w); c += sba_w
    vf_ref[...] = v
    vb_ref[...] = v.astype(BF16)
    gq_ref[...] = mm(c, gp_w) * gla_scale; c += gp_w
    gk_ref[...] = mm(c, gp_w); c += gp_w
    gv_ref[...] = mm(c, gvw); c += gvw
    gr_ref[...] = mm(c, gvw); c += gvw
    mq_ref[...] = (mm(c, mem_w) * mem_scale).astype(BF16); c += mem_w
    glow = mm(c, LANES)
    pre = jnp.dot(glow.astype(BF16), wg_ref[...], preferred_element_type=F32) + bg_ref[...]
    gg_ref[...] = -_softplus(-pre) * (1.0 / GLA_TAU)


def _project(x2d, w_packed, wg, bg, *, tm, sba_w, gp_w, gvw, mem_w, sba_d, gla_k, mem_d):
    m, d = x2d.shape
    npk = w_packed.shape[1]
    kern = functools.partial(_proj_kernel, sba_w=sba_w, gp_w=gp_w, gvw=gvw, mem_w=mem_w,
                             sba_scale=sba_d ** -0.5 * LOG2E, gla_scale=gla_k ** -0.5, mem_scale=mem_d ** -0.5)
    row = lambda w: pl.BlockSpec((tm, w), lambda i: (i, 0))
    const = lambda shape: pl.BlockSpec(shape, lambda i: (0, 0), pipeline_mode=pl.Buffered(1))
    out_shapes = [(sba_w, BF16), (sba_w, F32), (sba_w, F32), (sba_w, BF16), (sba_w, BF16),
                  (gp_w, F32), (gp_w, F32), (gvw, F32), (gvw, F32), (gp_w, F32), (mem_w, BF16)]
    return pl.pallas_call(
        kern,
        grid=(m // tm,),
        in_specs=[row(d), const((d, npk)), const(wg.shape), const(bg.shape)],
        out_specs=[row(w) for w, _ in out_shapes],
        out_shape=[jax.ShapeDtypeStruct((m, w), dt) for w, dt in out_shapes],
        compiler_params=_params(("parallel",)),
        name="in_proj",
    )(x2d, w_packed, wg, bg)


def _sba_tile(q_ref, k_ref, v_ref, u_ref, acc_ref, run_ref, bias, h, start, *, tq, hd, diag):
    lanes = slice(h * hd, (h + 1) * hd)
    k = k_ref[pl.ds(start, tq), lanes]
    v = v_ref[pl.ds(start, tq), lanes]
    z = lax.dot_general(q_ref[:, lanes], k, (((1,), (1,)), ((), ())), preferred_element_type=F32) + bias
    sp = jnp.maximum(z, 0.0) + jnp.log2(1.0 + jnp.exp2(-jnp.abs(z)))
    log_beta = z - sp
    if diag:
        keep = lax.broadcasted_iota(jnp.int32, (tq, tq), 1) < lax.broadcasted_iota(jnp.int32, (tq, tq), 0)
        sp = jnp.where(keep, sp, 0.0)
        log_beta = jnp.where(keep, log_beta, -1e30)
    ext = jnp.dot(sp.astype(BF16), u_ref[...], preferred_element_type=F32)
    rest, total = ext[:, :tq], ext[:, tq:]
    if diag:
        w = jnp.exp2(log_beta - rest)
        acc_ref[:, lanes] = jnp.dot(w.astype(BF16), v, preferred_element_type=F32)
        run_ref[h] = total
    else:
        run = run_ref[h]
        w = jnp.exp2(log_beta - rest - jnp.concatenate([run] * (tq // LANES), axis=1))
        acc_ref[:, lanes] += jnp.dot(w.astype(BF16), v, preferred_element_type=F32)
        run_ref[h] = run + total


def _sba_prompt_kernel(bias_ref, q_ref, k_ref, v_ref, u_ref, o_ref, acc_ref, run_ref, *, tq, hd, heads_per_step):
    i = pl.program_id(2)
    hg = pl.program_id(1)
    tile = functools.partial(_sba_tile, q_ref, k_ref, v_ref, u_ref, acc_ref, run_ref, tq=tq, hd=hd)
    biases = [bias_ref[hg * heads_per_step + h] for h in range(heads_per_step)]

    for h in range(heads_per_step):
        tile(biases[h], h, pl.multiple_of(i * tq, tq), diag=True)

    def body(jj, carry):
        start = pl.multiple_of((i - 1 - jj) * tq, tq)
        for h in range(heads_per_step):
            tile(biases[h], h, start, diag=False)
        return carry

    lax.fori_loop(0, i, body, 0)
    o_ref[...] = acc_ref[...].astype(o_ref.dtype)


def _sba_prompt(q, k, v, bias, *, batch, seq, heads, hd, tq=256, heads_per_step=8):
    m = q.shape[0]
    nq = seq // tq
    w = heads_per_step * hd
    strict = lax.broadcasted_iota(jnp.int32, (tq, tq), 0) > lax.broadcasted_iota(jnp.int32, (tq, tq), 1)
    u = jnp.concatenate([strict.astype(BF16), jnp.ones((tq, LANES), BF16)], axis=1)
    kern = functools.partial(_sba_prompt_kernel, tq=tq, hd=hd, heads_per_step=heads_per_step)
    return pl.pallas_call(
        kern,
        grid=(batch, heads // heads_per_step, nq),
        in_specs=[pl.BlockSpec(memory_space=pltpu.SMEM),
                  pl.BlockSpec((tq, w), lambda b, h, i: (b * nq + i, h)),
                  pl.BlockSpec((seq, w), lambda b, h, i: (b, h)),
                  pl.BlockSpec((seq, w), lambda b, h, i: (b, h)),
                  pl.BlockSpec(u.shape, lambda b, h, i: (0, 0))],
        out_specs=pl.BlockSpec((tq, w), lambda b, h, i: (b * nq + i, h)),
        out_shape=jax.ShapeDtypeStruct((m, heads * hd), BF16),
        scratch_shapes=[pltpu.VMEM((tq, w), F32), pltpu.VMEM((heads_per_step, tq, LANES), F32)],
        compiler_params=_params(("parallel", "parallel", "arbitrary")),
        name="sba_prompt",
    )(bias, q, k, v, u)


def _gla_gate_out(o, r, g):
    og = o * lax.rsqrt(jnp.mean(o * o, axis=-1, keepdims=True) + NORM_EPS)
    return og * g * (r * jax.nn.sigmoid(r))


GLA_FACTORISED_MAX_DECAY = 80.0


def _gla_chunk(refs, rows, st, tri, *, exact, intra_ref=None):
    q_ref, k_ref, v_ref, g_ref, r_ref, ng_ref, o_ref = refs
    chunk = tri.shape[0]
    q = q_ref[rows, :]
    k = k_ref[rows, :]
    v = v_ref[rows, :]
    gc = jnp.dot(tri, g_ref[rows, :], preferred_element_type=F32, precision=lax.Precision.HIGHEST)
    g_last = gc[chunk - 1:chunk, :]
    inter = lax.dot_general((q * jnp.exp(gc)).astype(BF16), st.astype(BF16),
                            (((1,), (1,)), ((), ())), preferred_element_type=F32)
    if not exact:
        g_mid = gc[chunk // 2 - 1:chunk // 2, :]
        qi = (q * jnp.exp(gc - g_mid)).astype(BF16)
        ki = (k * jnp.exp(g_mid - gc)).astype(BF16)
        scores = lax.dot_general(qi, ki, (((1,), (1,)), ((), ())), preferred_element_type=F32)
        scores = jnp.where(tri > 0.5, scores, 0.0)
        intra = jnp.dot(scores.astype(BF16), v.astype(BF16), preferred_element_type=F32)
    else:
        row_id = lax.broadcasted_iota(jnp.int32, (chunk, 1), 0)

        def row(i, carry):
            sel = row_id == i
            qrow = jnp.sum(jnp.where(sel, q, 0.0), axis=0, keepdims=True)
            grow = jnp.sum(jnp.where(sel, gc, 0.0), axis=0, keepdims=True)
            decay = jnp.exp(jnp.minimum(grow - gc, 0.0))
            s = jnp.sum(qrow * k * decay, axis=-1, keepdims=True)
            s = jnp.where(row_id <= i, s, 0.0)
            intra_ref[pl.ds(i, 1), :] = jnp.sum(s * v, axis=0, keepdims=True)
            return carry

        lax.fori_loop(0, chunk, row, 0)
        intra = intra_ref[...]
    o_ref[rows, :] = _gla_gate_out(inter + intra, r_ref[rows, :], ng_ref[...]).astype(o_ref.dtype)
    kd = (k * jnp.exp(g_last - gc)).astype(BF16)
    return st * jnp.exp(g_last) + lax.dot_general(v.astype(BF16), kd, (((0,), (0,)), ((), ())),
                                                  preferred_element_type=F32)


def _gla_prompt_kernel(q_ref, k_ref, v_ref, g_ref, r_ref, ng_ref, tri_ref, o_ref, s_ref, st_ref, intra_ref,
                       *, chunk, n_chunks, dk):
    refs = (q_ref, k_ref, v_ref, g_ref, r_ref, ng_ref, o_ref)
    tri = tri_ref[...]
    chunk_decay = jnp.sum(g_ref[...].reshape(n_chunks, chunk, LANES), axis=1)
    factorised_ok = jnp.min(chunk_decay) >= -GLA_FACTORISED_MAX_DECAY

    @pl.when(factorised_ok)
    def _():
        st = jnp.zeros((LANES, LANES), F32)
        for c in range(n_chunks):
            st = _gla_chunk(refs, slice(c * chunk, (c + 1) * chunk), st, tri, exact=False)
        st_ref[...] = st

    @pl.when(jnp.logical_not(factorised_ok))
    def _():
        def body(c, st):
            rows = pl.ds(pl.multiple_of(c * chunk, chunk), chunk)
            return _gla_chunk(refs, rows, st, tri, exact=True, intra_ref=intra_ref)
        st_ref[...] = lax.fori_loop(0, n_chunks, body, jnp.zeros((LANES, LANES), F32))

    s_ref[...] = st_ref[...].T[:dk, :]


def _gla_prompt(gq, gk, gv, gg, gr, norm_g, *, batch, seq, heads, dk, chunk=128):
    m = gq.shape[0]
    n_chunks = seq // chunk
    tri = (lax.broadcasted_iota(jnp.int32, (chunk, chunk), 0)
           >= lax.broadcasted_iota(jnp.int32, (chunk, chunk), 1)).astype(F32)
    kern = functools.partial(_gla_prompt_kernel, chunk=chunk, n_chunks=n_chunks, dk=dk)
    blk = pl.BlockSpec((seq, LANES), lambda b, h: (b, h))
    return pl.pallas_call(
        kern,
        grid=(batch, heads),
        in_specs=[blk, blk, blk, blk, blk,
                  pl.BlockSpec((1, LANES), lambda b, h: (0, h)),
                  pl.BlockSpec((chunk, chunk), lambda b, h: (0, 0))],
        out_specs=[blk, pl.BlockSpec((None, None, dk, LANES), lambda b, h: (b, h, 0, 0))],
        out_shape=[jax.ShapeDtypeStruct((m, heads * LANES), BF16),
                   jax.ShapeDtypeStruct((batch, heads, dk, LANES), F32)],
        scratch_shapes=[pltpu.VMEM((LANES, LANES), F32), pltpu.VMEM((chunk, LANES), F32)],
        compiler_params=_params(("parallel", "parallel")),
        name="gla_prompt",
    )(gq, gk, gv, gg, gr, norm_g, tri)


def _mem_kv_kernel(x_ref, wk_ref, wv_ref, kf_ref, vf_ref, kb_ref, vb_ref):
    xb = x_ref[...].astype(BF16)
    k = jnp.dot(xb, wk_ref[...], preferred_element_type=F32)
    v = jnp.dot(xb, wv_ref[...], preferred_element_type=F32)
    kf_ref[...] = k
    vf_ref[...] = v
    kb_ref[...] = k.astype(BF16)
    vb_ref[...] = v.astype(BF16)


def _mem_kv(mem2d, wk, wv, *, tm=512):
    m, d = mem2d.shape
    w = wk.shape[1]
    row = lambda width: pl.BlockSpec((tm, width), lambda i: (i, 0))
    const = pl.BlockSpec((d, w), lambda i: (0, 0))
    return pl.pallas_call(
        _mem_kv_kernel,
        grid=(m // tm,),
        in_specs=[row(d), const, const],
        out_specs=[row(w)] * 4,
        out_shape=[jax.ShapeDtypeStruct((m, w), F32)] * 2 + [jax.ShapeDtypeStruct((m, w), BF16)] * 2,
        compiler_params=_params(("parallel",)),
        name="mem_kv",
    )(mem2d, wk, wv)


def _softmax_rows(s):
    s = s - jnp.max(s, axis=-1, keepdims=True)
    p = jnp.exp(s)
    return p / jnp.sum(p, axis=-1, keepdims=True)


def _wo_ln_kernel(*refs, alpha, sba_w, gvw, mem_heads, hd, fuse_mem):
    if fuse_mem:
        x_ref, a_ref, g_ref, mq_ref, mk_ref, mv_ref, wo_ref, lg_ref, lb_ref, o_ref = refs
    else:
        x_ref, a_ref, g_ref, om_ref, wo_ref, lg_ref, lb_ref, o_ref = refs
    acc = alpha * x_ref[...]
    acc = acc + jnp.dot(a_ref[...], wo_ref[0:sba_w, :], preferred_element_type=F32)
    acc = acc + jnp.dot(g_ref[...], wo_ref[sba_w:sba_w + gvw, :], preferred_element_type=F32)
    base = sba_w + gvw
    if fuse_mem:
        om = []
        for h in range(mem_heads):
            lanes = slice(h * hd, (h + 1) * hd)
            s = lax.dot_general(mq_ref[:, lanes], mk_ref[:, lanes], (((1,), (1,)), ((), ())),
                                preferred_element_type=F32)
            p = _softmax_rows(s)
            om.append(jnp.dot(p.astype(BF16), mv_ref[:, lanes], preferred_element_type=F32).astype(BF16))
        acc = acc + jnp.dot(jnp.concatenate(om, axis=1), wo_ref[base:, :], preferred_element_type=F32)
    else:
        acc = acc + jnp.dot(om_ref[...], wo_ref[base:, :], preferred_element_type=F32)
    o_ref[...] = _layer_norm(acc, lg_ref[...], lb_ref[...])


def _wo_ln(x2d, o_sba, og, mem_args, wo, ln_g, ln_b, *, tm, alpha, mem_heads, hd, rows_per_batch=None):
    m, d = x2d.shape
    sba_w, gvw = o_sba.shape[1], og.shape[1]
    fuse_mem = rows_per_batch is not None
    row = lambda w: pl.BlockSpec((tm, w), lambda i: (i, 0))
    const = lambda shape: pl.BlockSpec(shape, lambda i: (0, 0), pipeline_mode=pl.Buffered(1))
    if fuse_mem:
        mq, mk, mv = mem_args
        n_mem = mk.shape[0] // (m // rows_per_batch)
        tiles_per_batch = rows_per_batch // tm
        per_batch = pl.BlockSpec((n_mem, mk.shape[1]), lambda i: (i // tiles_per_batch, 0))
        mem_specs = [row(mq.shape[1]), per_batch, per_batch]
    else:
        mem_specs = [row(mem_args[0].shape[1])]
    kern = functools.partial(_wo_ln_kernel, alpha=alpha, sba_w=sba_w, gvw=gvw, mem_heads=mem_heads, hd=hd,
                             fuse_mem=fuse_mem)
    return pl.pallas_call(
        kern,
        grid=(m // tm,),
        in_specs=[row(d), row(sba_w), row(gvw)] + mem_specs + [const(wo.shape), const(ln_g.shape), const(ln_b.shape)],
        out_specs=row(d),
        out_shape=jax.ShapeDtypeStruct((m, d), F32),
        compiler_params=_params(("parallel",)),
        name="wo_ln1",
    )(x2d, o_sba, og, *mem_args, wo, ln_g, ln_b)


def _mlp_init(x_ref, xb_ref, acc_ref):
    xb_ref[...] = x_ref[...].astype(BF16)
    acc_ref[...] = jnp.zeros_like(acc_ref)


def _mlp_accumulate(xb_ref, wu_ref, wd_ref, acc_ref):
    h = jnp.dot(xb_ref[...], wu_ref[...], preferred_element_type=F32)
    u = jnp.square(jnp.maximum(h, 0.0))
    acc_ref[...] += jnp.dot(u.astype(BF16), wd_ref[...], preferred_element_type=F32)


def _mlp_finish(x_ref, acc_ref, lg_ref, lb_ref, o_ref, alpha):
    o_ref[...] = _layer_norm(alpha * x_ref[...] + acc_ref[...], lg_ref[...], lb_ref[...])


def _mlp_kernel(x_ref, wu_ref, wd_ref, lg_ref, lb_ref, o_ref, xb_ref, acc_ref, *, alpha):
    f = pl.program_id(1)

    @pl.when(f == 0)
    def _():
        _mlp_init(x_ref, xb_ref, acc_ref)

    _mlp_accumulate(xb_ref, wu_ref, wd_ref, acc_ref)

    @pl.when(f == pl.num_programs(1) - 1)
    def _():
        _mlp_finish(x_ref, acc_ref, lg_ref, lb_ref, o_ref, alpha)


def _mlp_ln(x1, wu, wd, ln_g, ln_b, *, tm, tf, alpha):
    m, d = x1.shape
    dff = wu.shape[1]
    kern = functools.partial(_mlp_kernel, alpha=alpha)
    return pl.pallas_call(
        kern,
        grid=(m // tm, dff // tf),
        in_specs=[pl.BlockSpec((tm, d), lambda i, f: (i, 0)),
                  pl.BlockSpec((d, tf), lambda i, f: (0, f)),
                  pl.BlockSpec((tf, d), lambda i, f: (f, 0)),
                  pl.BlockSpec((1, d), lambda i, f: (0, 0)),
                  pl.BlockSpec((1, d), lambda i, f: (0, 0))],
        out_specs=pl.BlockSpec((tm, d), lambda i, f: (i, 0)),
        out_shape=jax.ShapeDtypeStruct((m, d), F32),
        scratch_shapes=[pltpu.VMEM((tm, d), BF16), pltpu.VMEM((tm, d), F32)],
        compiler_params=_params(("parallel", "arbitrary")),
        name="mlp_ln2",
    )(x1, wu, wd, ln_g, ln_b)


def _decode_init(acc_ref, run_ref):
    acc_ref[...] = jnp.zeros_like(acc_ref)
    run_ref[...] = jnp.zeros_like(run_ref)


def _dot_2pass(a, b):
    hi = a.astype(BF16)
    lo = (a - hi.astype(F32)).astype(BF16)
    return jnp.dot(hi, b, preferred_element_type=F32) + jnp.dot(lo, b, preferred_element_type=F32)


def _decode_accumulate(bias_ref, q_ref, k_refs, v_refs, u_ref, newer_ref, acc_ref, run_ref, heads):
    pages_per_step = len(k_refs)
    groups = k_refs[0].shape[0] // LANES
    lane_head = lax.broadcasted_iota(jnp.int32, (heads, LANES), 1) & (heads - 1)
    own = lane_head == lax.broadcasted_iota(jnp.int32, (heads, LANES), 0)
    q = q_ref[...].astype(F32)
    qt = lax.dot_general(q, own.astype(F32), (((0,), (0,)), ((), ())), preferred_element_type=F32)
    rows = []
    for j in range(pages_per_step):
        for g in reversed(range(groups)):
            kt = k_refs[j][g * LANES:(g + 1) * LANES, :].T
            rows.append(jnp.sum(kt * qt, axis=0, keepdims=True))
    z = (jnp.concatenate(rows, axis=0) + bias_ref[...]) * LN2
    sp = _softplus(z)
    log_keep = -sp
    log_beta = z - sp
    ext = _dot_2pass(log_keep, u_ref[...])
    incl, total = ext[:, :LANES], ext[:, LANES:]
    run = run_ref[...]
    tot_hi = total.astype(BF16)
    tot_lo = (total - tot_hi.astype(F32)).astype(BF16)
    newer = (jnp.dot(newer_ref[...], tot_hi, preferred_element_type=F32)
             + jnp.dot(newer_ref[...], tot_lo, preferred_element_type=F32))
    wgt = jnp.exp(log_beta + (incl - log_keep) + newer + run)
    run_ref[...] = run + jnp.sum(total, axis=0, keepdims=True)
    acc = acc_ref[...]
    for j in range(pages_per_step):
        first = j * groups
        wj = jnp.concatenate(
            [jnp.where(own, jnp.broadcast_to(wgt[first + groups - 1 - g:first + groups - g], own.shape), 0.0)
             for g in range(groups)], axis=1)
        acc = acc + jnp.dot(wj, v_refs[j][...], preferred_element_type=F32)
    acc_ref[...] = acc


def _sba_decode_kernel(pt_ref, bias_ref, q_ref, *refs, pages_per_step, heads):
    k_refs = refs[:pages_per_step]
    v_refs = refs[pages_per_step:2 * pages_per_step]
    u_ref, newer_ref, o_ref, acc_ref, run_ref = refs[2 * pages_per_step:]
    c = pl.program_id(1)

    @pl.when(c == 0)
    def _():
        _decode_init(acc_ref, run_ref)

    _decode_accumulate(bias_ref, q_ref, k_refs, v_refs, u_ref, newer_ref, acc_ref, run_ref, heads)

    @pl.when(c == pl.num_programs(1) - 1)
    def _():
        o_ref[...] = acc_ref[...]


def _mlp_decode_kernel(pt_ref, x_ref, wu_ref, wd_ref, lg_ref, lb_ref, bias_ref, q_ref, *refs,
                       alpha, pages_per_step, heads):
    k_refs = refs[:pages_per_step]
    v_refs = refs[pages_per_step:2 * pages_per_step]
    u_ref, newer_ref, o_ref, osba_ref, xb_ref, acc_ref, dacc_ref, drun_ref = refs[2 * pages_per_step:]
    f = pl.program_id(1)

    @pl.when(f == 0)
    def _():
        _mlp_init(x_ref, xb_ref, acc_ref)
        _decode_init(dacc_ref, drun_ref)

    _mlp_accumulate(xb_ref, wu_ref, wd_ref, acc_ref)
    _decode_accumulate(bias_ref, q_ref, k_refs, v_refs, u_ref, newer_ref, dacc_ref, drun_ref, heads)

    @pl.when(f == pl.num_programs(1) - 1)
    def _():
        _mlp_finish(x_ref, acc_ref, lg_ref, lb_ref, o_ref, alpha)
        osba_ref[...] = dacc_ref[...]


def _decode_operands(q, cache_k, cache_v, layer, page_table, bias, pages_per_step):
    n_seq, n_pages = page_table.shape
    depth, n_phys, page, heads, hd = cache_k.shape
    assert heads & (heads - 1) == 0 and LANES % heads == 0 and (page * heads) % LANES == 0
    rows = page * heads
    ck = cache_k.reshape(depth * n_phys, rows, hd)
    cv = cache_v.reshape(depth * n_phys, rows, hd)
    src = lax.broadcasted_iota(jnp.int32, (LANES, LANES), 0)
    dst = lax.broadcasted_iota(jnp.int32, (LANES, LANES), 1)
    same_head = ((src - dst) & (heads - 1)) == 0
    u = jnp.concatenate([(same_head & (src >= dst)).astype(BF16), same_head.astype(BF16)], axis=1)
    n_blocks = pages_per_step * rows // LANES
    newer = (lax.broadcasted_iota(jnp.int32, (n_blocks, n_blocks), 1)
             < lax.broadcasted_iota(jnp.int32, (n_blocks, n_blocks), 0)).astype(BF16)
    bias_row = jnp.tile(bias.astype(F32), LANES // heads).reshape(1, LANES)

    def page_spec(j):
        def index(b, c, pt):
            logical = n_pages - 1 - (c * pages_per_step + j)
            return (layer * n_phys + pt[b * n_pages + logical], 0, 0)
        return pl.BlockSpec((None, rows, hd), index)

    const = lambda a: pl.BlockSpec(a.shape, lambda b, c, pt: (0, 0))
    seq_spec = pl.BlockSpec((None, heads, hd), lambda b, c, pt: (b, 0, 0))
    head_specs = [const(bias_row), seq_spec]
    page_specs = [page_spec(j) for j in range(pages_per_step)] * 2 + [const(u), const(newer)]
    head_args = (bias_row, q.reshape(n_seq, heads, hd))
    page_args = (*([ck] * pages_per_step), *([cv] * pages_per_step), u, newer)
    scratch = [pltpu.VMEM((heads, hd), F32), pltpu.VMEM((1, LANES), F32)]
    return head_specs, page_specs, head_args, page_args, seq_spec, scratch


def _sba_decode(q, cache_k, cache_v, layer, page_table, bias, *, pages_per_step=16):
    n_seq, n_pages = page_table.shape
    heads, hd = cache_k.shape[3:]
    head_specs, page_specs, head_args, page_args, seq_spec, scratch = _decode_operands(
        q, cache_k, cache_v, layer, page_table, bias, pages_per_step)
    kern = functools.partial(_sba_decode_kernel, pages_per_step=pages_per_step, heads=heads)
    grid_spec = pltpu.PrefetchScalarGridSpec(
        num_scalar_prefetch=1,
        grid=(n_seq, n_pages // pages_per_step),
        in_specs=head_specs + page_specs,
        out_specs=seq_spec,
        scratch_shapes=scratch,
    )
    out = pl.pallas_call(
        kern,
        grid_spec=grid_spec,
        out_shape=jax.ShapeDtypeStruct((n_seq, heads, hd), F32),
        compiler_params=_params(("parallel", "arbitrary")),
        name="sba_decode",
    )(page_table.reshape(-1), *head_args, *page_args)
    return out.reshape(n_seq, heads * hd)


SHARED_GRID_MAX_TM = 512
SHARED_GRID_TF = 512


def _shared_grid_tiles(m, dff, n_seq, n_pages):
    if m % n_seq or dff % SHARED_GRID_TF:
        return None, None
    tm, col_steps = m // n_seq, dff // SHARED_GRID_TF
    if tm % 8 or tm > SHARED_GRID_MAX_TM or n_pages % col_steps:
        return None, None
    return tm, SHARED_GRID_TF


def _mlp_ln_with_decode(x1, wu, wd, ln_g, ln_b, q, cache_k, cache_v, layer, page_table, bias, *, tm, tf, alpha):
    m, d = x1.shape
    dff = wu.shape[1]
    n_seq, n_pages = page_table.shape
    heads, hd = cache_k.shape[3:]
    grid = (m // tm, dff // tf)
    assert grid[0] == n_seq and n_pages % grid[1] == 0
    pages_per_step = n_pages // grid[1]
    head_specs, page_specs, head_args, page_args, seq_spec, scratch = _decode_operands(
        q, cache_k, cache_v, layer, page_table, bias, pages_per_step)
    kern = functools.partial(_mlp_decode_kernel, alpha=alpha, pages_per_step=pages_per_step, heads=heads)
    grid_spec = pltpu.PrefetchScalarGridSpec(
        num_scalar_prefetch=1,
        grid=grid,
        in_specs=[pl.BlockSpec((tm, d), lambda i, f, pt: (i, 0)),
                  pl.BlockSpec((d, tf), lambda i, f, pt: (0, f)),
                  pl.BlockSpec((tf, d), lambda i, f, pt: (f, 0)),
                  pl.BlockSpec((1, d), lambda i, f, pt: (0, 0)),
                  pl.BlockSpec((1, d), lambda i, f, pt: (0, 0))] + head_specs + page_specs,
        out_specs=[pl.BlockSpec((tm, d), lambda i, f, pt: (i, 0)), seq_spec],
        scratch_shapes=[pltpu.VMEM((tm, d), BF16), pltpu.VMEM((tm, d), F32)] + scratch,
    )
    y, o_sba = pl.pallas_call(
        kern,
        grid_spec=grid_spec,
        out_shape=[jax.ShapeDtypeStruct((m, d), F32), jax.ShapeDtypeStruct((n_seq, heads, hd), F32)],
        compiler_params=_params(("parallel", "arbitrary")),
        name="mlp_ln2_sba_decode",
    )(page_table.reshape(-1), x1, wu, wd, ln_g, ln_b, *head_args, *page_args)
    return y, o_sba.reshape(n_seq, heads * hd)


def _gla_step_kernel(q_ref, k_ref, g_ref, v_ref, r_ref, ng_ref, s_ref, o_ref, sn_ref):
    s_new = jnp.exp(g_ref[...]) * s_ref[...].astype(F32) + k_ref[...] * v_ref[...]
    sn_ref[...] = s_new
    o = jnp.sum(q_ref[...] * s_new, axis=2, keepdims=True)
    o_ref[...] = _gla_gate_out(o, r_ref[...], ng_ref[...]).astype(o_ref.dtype)


def _gla_step(q_col, k_col, g_col, v_row, r_row, norm_g, state, layer, *, bb=8):
    depth, n_seq, heads, dk, dv = state.shape
    steps = n_seq // bb
    col = pl.BlockSpec((bb, heads, dk, 1), lambda i: (i, 0, 0, 0))
    rowv = pl.BlockSpec((bb, heads, 1, dv), lambda i: (i, 0, 0, 0))
    st = pl.BlockSpec((bb, heads, dk, dv), lambda i: (i, 0, 0, 0))
    st_in = pl.BlockSpec((bb, heads, dk, dv), lambda i: (layer * steps + i, 0, 0, 0))
    return pl.pallas_call(
        _gla_step_kernel,
        grid=(steps,),
        in_specs=[col, col, col, rowv, rowv, pl.BlockSpec((1, heads, 1, dv), lambda i: (0, 0, 0, 0)), st_in],
        out_specs=[rowv, st],
        out_shape=[jax.ShapeDtypeStruct((n_seq, heads, 1, dv), BF16),
                   jax.ShapeDtypeStruct((n_seq, heads, dk, dv), F32)],
        compiler_params=_params(("parallel",)),
        name="gla_step",
    )(q_col, k_col, g_col, v_row, r_row, norm_g, state.reshape(depth * n_seq, heads, dk, dv))


def _mem_decode_kernel(q_ref, k_ref, v_ref, o_ref, *, heads, seqs):
    rows = k_ref.shape[1]
    own = ((lax.broadcasted_iota(jnp.int32, (heads, rows), 1) & (heads - 1))
           == lax.broadcasted_iota(jnp.int32, (heads, rows), 0))
    for b in range(seqs):
        s = lax.dot_general(q_ref[b].astype(F32), k_ref[b], (((1,), (1,)), ((), ())), preferred_element_type=F32)
        s = jnp.where(own, s, -1e30)
        p = jnp.where(own, jnp.exp(s - jnp.max(s, axis=-1, keepdims=True)), 0.0)
        o = jnp.dot(p, v_ref[b], preferred_element_type=F32)
        o_ref[b] = (o / jnp.sum(p, axis=-1, keepdims=True)).astype(o_ref.dtype)


def _mem_decode(mq, mem_k, mem_v, layer, *, seqs_per_step=8):
    depth, n_seq, n_mem, heads, hd = mem_k.shape
    assert heads & (heads - 1) == 0
    rows = n_mem * heads
    steps = n_seq // seqs_per_step
    kern = functools.partial(_mem_decode_kernel, heads=heads, seqs=seqs_per_step)
    vec = pl.BlockSpec((seqs_per_step, heads, hd), lambda i: (i, 0, 0))
    kv = pl.BlockSpec((seqs_per_step, rows, hd), lambda i: (layer * steps + i, 0, 0))
    out = pl.pallas_call(
        kern,
        grid=(steps,),
        in_specs=[vec, kv, kv],
        out_specs=vec,
        out_shape=jax.ShapeDtypeStruct((n_seq, heads, hd), BF16),
        compiler_params=_params(("parallel",)),
        name="mem_decode",
    )(mq.reshape(n_seq, heads, hd), mem_k.reshape(depth * n_seq, rows, hd), mem_v.reshape(depth * n_seq, rows, hd))
    return out.reshape(n_seq, heads * hd)


def kernel(x_prompt, x_sample, cache_sba_k, cache_sba_v, state_gla, cache_mem_k, cache_mem_v, page_table,
           mem_prompt, w_in, sba_bias, w_gate_up, b_gate, gla_norm_g, w_mem_k, w_mem_v, w_o, ln1_g, ln1_b,
           w_up, w_down, ln2_g, ln2_b):
    depth = w_in.shape[0]
    bp, tp, d = x_prompt.shape
    bs, ts, _ = x_sample.shape
    assert ts == 1, "the decode path handles one new token per sequence"
    _, n_phys, page, sba_h, sba_d = cache_sba_k.shape
    _, _, gla_h, gla_k, gla_v = state_gla.shape
    _, _, n_mem, mem_h, mem_d = cache_mem_k.shape
    rank = w_gate_up.shape[1]
    assert gla_v == LANES and gla_k <= LANES and sba_d == LANES and mem_d == LANES
    sba_w, gkw, gvw, mem_w = sba_h * sba_d, gla_h * gla_k, gla_h * gla_v, mem_h * mem_d
    gp_w = gla_h * LANES
    alpha = (2.0 * depth) ** 0.25
    dims = (sba_w, gkw, gvw, rank, mem_w, gla_h, gla_k)
    proj = functools.partial(_project, sba_w=sba_w, gp_w=gp_w, gvw=gvw, mem_w=mem_w,
                             sba_d=sba_d, gla_k=gla_k, mem_d=mem_d)

    yp = x_prompt.reshape(bp * tp, d)
    ys = x_sample.reshape(bs * ts, d)
    outs = [[] for _ in range(8)]
    for l in range(depth):
        w_packed, wg, bg = _pack_w_in(w_in[l], w_gate_up[l], b_gate[l], dims)
        wo = w_o[l].astype(BF16)
        wu = w_up[l].astype(BF16)
        wd = w_down[l].astype(BF16)
        ng = gla_norm_g[l].reshape(1, gvw)
        l1g, l1b = ln1_g[l].reshape(1, d), ln1_b[l].reshape(1, d)
        l2g, l2b = ln2_g[l].reshape(1, d), ln2_b[l].reshape(1, d)
        bias = sba_bias[l].astype(F32)

        q, kf, vf, kb, vb, gq, gk, gv, gr, gg, mq = proj(yp, w_packed, wg, bg, tm=256)
        o_sba = _sba_prompt(q, kb, vb, bias * LOG2E, batch=bp, seq=tp, heads=sba_h, hd=sba_d)
        og, s_p = _gla_prompt(gq, gk, gv, gg, gr, ng, batch=bp, seq=tp, heads=gla_h, dk=gla_k)
        mkf, mvf, mkb, mvb = _mem_kv(mem_prompt.reshape(bp * n_mem, d), w_mem_k[l].astype(BF16),
                                     w_mem_v[l].astype(BF16))
        x1 = _wo_ln(yp, o_sba, og, (mq, mkb, mvb), wo, l1g, l1b, tm=512, alpha=alpha, mem_heads=mem_h, hd=mem_d,
                    rows_per_batch=tp)
        outs[0].append(kf.reshape(bp, tp, sba_h, sba_d))
        outs[1].append(vf.reshape(bp, tp, sba_h, sba_d))
        outs[2].append(s_p)
        outs[3].append(mkf.reshape(bp, n_mem, mem_h, mem_d))
        outs[4].append(mvf.reshape(bp, n_mem, mem_h, mem_d))

        q2, kf2, vf2, _, _, gq2, gk2, gv2, gr2, gg2, mq2 = proj(ys, w_packed, wg, bg, tm=bs)
        decode_args = (q2, cache_sba_k, cache_sba_v, l, page_table, bias * LOG2E)
        tm_shared, tf_shared = _shared_grid_tiles(bp * tp, wu.shape[1], *page_table.shape)
        if tm_shared is not None:
            yp_next, o_sba2 = _mlp_ln_with_decode(x1, wu, wd, l2g, l2b, *decode_args, tm=tm_shared, tf=tf_shared,
                                                  alpha=alpha)
        else:
            yp_next = _mlp_ln(x1, wu, wd, l2g, l2b, tm=512, tf=1024, alpha=alpha)
            o_sba2 = _sba_decode(*decode_args)
        o_sba2 = o_sba2.astype(BF16)
        col = lambda a: a.reshape(bs, gla_h, LANES)[:, :, :gla_k].reshape(bs, gla_h, gla_k, 1)
        rowv = lambda a: a.reshape(bs, gla_h, 1, gla_v)
        og2, s_s = _gla_step(col(gq2), col(gk2), col(gg2), rowv(gv2), rowv(gr2), ng.reshape(1, gla_h, 1, gla_v),
                             state_gla, l)
        o_mem2 = _mem_decode(mq2, cache_mem_k, cache_mem_v, l)
        x1s = _wo_ln(ys, o_sba2, og2.reshape(bs, gvw), (o_mem2,), wo, l1g, l1b, tm=bs, alpha=alpha,
                     mem_heads=mem_h, hd=mem_d)
        ys_next = _mlp_ln(x1s, wu, wd, l2g, l2b, tm=bs, tf=1024, alpha=alpha)
        outs[5].append(kf2.reshape(bs, ts, sba_h, sba_d))
        outs[6].append(vf2.reshape(bs, ts, sba_h, sba_d))
        outs[7].append(s_s)

        yp, ys = yp_next, ys_next

    return (yp.reshape(bp, tp, d), ys.reshape(bs, ts, d), *[jnp.stack(o) for o in outs])
```

```python
import functools
import math

import jax
import jax.numpy as jnp
from jax import lax
from jax.experimental import pallas as pl
from jax.experimental.pallas import tpu as pltpu

F32 = jnp.float32
BF16 = jnp.bfloat16

LANES = 128
LN_EPS = 1e-5
NORM_EPS = 1e-6
GLA_TAU = 16.0
LOG2E = math.log2(math.e)
LN2 = math.log(2.0)
VMEM_LIMIT_BYTES = 56 * 1024 * 1024


def _params(sem):
    return pltpu.CompilerParams(dimension_semantics=sem, vmem_limit_bytes=VMEM_LIMIT_BYTES)


def _softplus(z):
    return jnp.maximum(z, 0.0) + jnp.log1p(jnp.exp(-jnp.abs(z)))


def _layer_norm(v, g, b):
    mu = jnp.mean(v, axis=-1, keepdims=True)
    vc = v - mu
    var = jnp.mean(vc * vc, axis=-1, keepdims=True)
    return vc * lax.rsqrt(var + LN_EPS) * g + b


def _pack_w_in(w_in, w_gate_up, b_gate, dims):
    sba_w, gkw, gvw, rank, mem_w, gh, gk = dims
    o = [0]
    for s in (sba_w, sba_w, sba_w, gkw, gkw, gvw, gvw, rank, mem_w):
        o.append(o[-1] + s)
    d = w_in.shape[0]

    def pad_heads(w):
        w = w.reshape(d, gh, gk)
        return jnp.pad(w, ((0, 0), (0, 0), (0, LANES - gk))).reshape(d, gh * LANES)

    cols = [w_in[:, o[0]:o[3]], pad_heads(w_in[:, o[3]:o[4]]), pad_heads(w_in[:, o[4]:o[5]]),
            w_in[:, o[5]:o[7]], w_in[:, o[8]:o[9]],
            jnp.pad(w_in[:, o[7]:o[8]], ((0, 0), (0, LANES - rank)))]
    w_packed = jnp.concatenate(cols, axis=1).astype(BF16)
    wg = jnp.pad(w_gate_up.reshape(rank, gh, gk), ((0, LANES - rank), (0, 0), (0, LANES - gk)))
    wg = wg.reshape(LANES, gh * LANES).astype(BF16)
    bg = jnp.pad(b_gate.reshape(1, gh, gk), ((0, 0), (0, 0), (0, LANES - gk))).reshape(1, gh * LANES)
    return w_packed, wg, bg.astype(F32)


def _proj_kernel(x_ref, w_ref, wg_ref, bg_ref,
                 q_ref, kf_ref, vf_ref, kb_ref, vb_ref, gq_ref, gk_ref, gv_ref, gr_ref, gg_ref, mq_ref,
                 *, sba_w, gp_w, gvw, mem_w, sba_scale, gla_scale, mem_scale):
    xb = x_ref[...].astype(BF16)

    def mm(lo, width):
        return jnp.dot(xb, w_ref[:, lo:lo + width], preferred_element_type=F32)

    c = 0
    q_ref[...] = (mm(c, sba_w) * sba_scale).astype(BF16); c += sba_w
    k = mm(c, sba_w); c += sba_w
    kf_ref[...] = k
    kb_ref[...] = k.astype(BF16)
    v = mm(c, sba_w); c += sba_w
    vf_ref[...] = v
    vb_ref[...] = v.astype(BF16)
    gq_ref[...] = mm(c, gp_w) * gla_scale; c += gp_w
    gk_ref[...] = mm(c, gp_w); c += gp_w
    gv_ref[...] = mm(c, gvw); c += gvw
    gr_ref[...] = mm(c, gvw); c += gvw
    mq_ref[...] = (mm(c, mem_w) * mem_scale).astype(BF16); c += mem_w
    glow = mm(c, LANES)
    pre = jnp.dot(glow.astype(BF16), wg_ref[...], preferred_element_type=F32) + bg_ref[...]
    gg_ref[...] = -_softplus(-pre) * (1.0 / GLA_TAU)


def _project(x2d, w_packed, wg, bg, *, tm, sba_w, gp_w, gvw, mem_w, sba_d, gla_k, mem_d):
    m, d = x2d.shape
    npk = w_packed.shape[1]
    kern = functools.partial(_proj_kernel, sba_w=sba_w, gp_w=gp_w, gvw=gvw, mem_w=mem_w,
                             sba_scale=sba_d ** -0.5 * LOG2E, gla_scale=gla_k ** -0.5, mem_scale=mem_d ** -0.5)
    row = lambda w: pl.BlockSpec((tm, w), lambda i: (i, 0))
    const = lambda shape: pl.BlockSpec(shape, lambda i: (0, 0), pipeline_mode=pl.Buffered(1))
    out_shapes = [(sba_w, BF16), (sba_w, F32), (sba_w, F32), (sba_w, BF16), (sba_w, BF16),
                  (gp_w, F32), (gp_w, F32), (gvw, F32), (gvw, F32), (gp_w, F32), (mem_w, BF16)]
    return pl.pallas_call(
        kern,
        grid=(m // tm,),
        in_specs=[row(d), const((d, npk)), const(wg.shape), const(bg.shape)],
        out_specs=[row(w) for w, _ in out_shapes],
        out_shape=[jax.ShapeDtypeStruct((m, w), dt) for w, dt in out_shapes],
        compiler_params=_params(("parallel",)),
        name="in_proj",
    )(x2d, w_packed, wg, bg)


def _sba_tile(q_ref, k_ref, v_ref, u_ref, acc_ref, run_ref, bias, h, start, *, tq, hd, diag):
    lanes = slice(h * hd, (h + 1) * hd)
    k = k_ref[pl.ds(start, tq), lanes]
    v = v_ref[pl.ds(start, tq), lanes]
    z = lax.dot_general(q_ref[:, lanes], k, (((1,), (1,)), ((), ())), preferred_element_type=F32) + bias
    sp = jnp.maximum(z, 0.0) + jnp.log2(1.0 + jnp.exp2(-jnp.abs(z)))
    log_beta = z - sp
    if diag:
        keep = lax.broadcasted_iota(jnp.int32, (tq, tq), 1) < lax.broadcasted_iota(jnp.int32, (tq, tq), 0)
        sp = jnp.where(keep, sp, 0.0)
        log_beta = jnp.where(keep, log_beta, -1e30)
    ext = jnp.dot(sp.astype(BF16), u_ref[...], preferred_element_type=F32)
    rest, total = ext[:, :tq], ext[:, tq:]
    if diag:
        w = jnp.exp2(log_beta - rest)
        acc_ref[:, lanes] = jnp.dot(w.astype(BF16), v, preferred_element_type=F32)
        run_ref[h] = total
    else:
        run = run_ref[h]
        w = jnp.exp2(log_beta - rest - jnp.concatenate([run] * (tq // LANES), axis=1))
        acc_ref[:, lanes] += jnp.dot(w.astype(BF16), v, preferred_element_type=F32)
        run_ref[h] = run + total


def _sba_prompt_kernel(bias_ref, q_ref, k_ref, v_ref, u_ref, o_ref, acc_ref, run_ref, *, tq, hd, heads_per_step):
    i = pl.program_id(2)
    hg = pl.program_id(1)
    tile = functools.partial(_sba_tile, q_ref, k_ref, v_ref, u_ref, acc_ref, run_ref, tq=tq, hd=hd)
    biases = [bias_ref[hg * heads_per_step + h] for h in range(heads_per_step)]

    for h in range(heads_per_step):
        tile(biases[h], h, pl.multiple_of(i * tq, tq), diag=True)

    def body(jj, carry):
        start = pl.multiple_of((i - 1 - jj) * tq, tq)
        for h in range(heads_per_step):
            tile(biases[h], h, start, diag=False)
        return carry

    lax.fori_loop(0, i, body, 0)
    o_ref[...] = acc_ref[...].astype(o_ref.dtype)


def _sba_prompt(q, k, v, bias, *, batch, seq, heads, hd, tq=256, heads_per_step=8):
    m = q.shape[0]
    nq = seq // tq
    w = heads_per_step * hd
    strict = lax.broadcasted_iota(jnp.int32, (tq, tq), 0) > lax.broadcasted_iota(jnp.int32, (tq, tq), 1)
    u = jnp.concatenate([strict.astype(BF16), jnp.ones((tq, LANES), BF16)], axis=1)
    kern = functools.partial(_sba_prompt_kernel, tq=tq, hd=hd, heads_per_step=heads_per_step)
    return pl.pallas_call(
        kern,
        grid=(batch, heads // heads_per_step, nq),
        in_specs=[pl.BlockSpec(memory_space=pltpu.SMEM),
                  pl.BlockSpec((tq, w), lambda b, h, i: (b * nq + i, h)),
                  pl.BlockSpec((seq, w), lambda b, h, i: (b, h)),
                  pl.BlockSpec((seq, w), lambda b, h, i: (b, h)),
                  pl.BlockSpec(u.shape, lambda b, h, i: (0, 0))],
        out_specs=pl.BlockSpec((tq, w), lambda b, h, i: (b * nq + i, h)),
        out_shape=jax.ShapeDtypeStruct((m, heads * hd), BF16),
        scratch_shapes=[pltpu.VMEM((tq, w), F32), pltpu.VMEM((heads_per_step, tq, LANES), F32)],
        compiler_params=_params(("parallel", "parallel", "arbitrary")),
        name="sba_prompt",
    )(bias, q, k, v, u)


def _gla_gate_out(o, r, g):
    og = o * lax.rsqrt(jnp.mean(o * o, axis=-1, keepdims=True) + NORM_EPS)
    return og * g * (r * jax.nn.sigmoid(r))


GLA_FACTORISED_MAX_DECAY = 80.0


def _gla_chunk(refs, rows, st, tri, *, exact, intra_ref=None):
    q_ref, k_ref, v_ref, g_ref, r_ref, ng_ref, o_ref = refs
    chunk = tri.shape[0]
    q = q_ref[rows, :]
    k = k_ref[rows, :]
    v = v_ref[rows, :]
    gc = jnp.dot(tri, g_ref[rows, :], preferred_element_type=F32, precision=lax.Precision.HIGHEST)
    g_last = gc[chunk - 1:chunk, :]
    inter = lax.dot_general((q * jnp.exp(gc)).astype(BF16), st.astype(BF16),
                            (((1,), (1,)), ((), ())), preferred_element_type=F32)
    if not exact:
        g_mid = gc[chunk // 2 - 1:chunk // 2, :]
        qi = (q * jnp.exp(gc - g_mid)).astype(BF16)
        ki = (k * jnp.exp(g_mid - gc)).astype(BF16)
        scores = lax.dot_general(qi, ki, (((1,), (1,)), ((), ())), preferred_element_type=F32)
        scores = jnp.where(tri > 0.5, scores, 0.0)
        intra = jnp.dot(scores.astype(BF16), v.astype(BF16), preferred_element_type=F32)
    else:
        row_id = lax.broadcasted_iota(jnp.int32, (chunk, 1), 0)

        def row(i, carry):
            sel = row_id == i
            qrow = jnp.sum(jnp.where(sel, q, 0.0), axis=0, keepdims=True)
            grow = jnp.sum(jnp.where(sel, gc, 0.0), axis=0, keepdims=True)
            decay = jnp.exp(jnp.minimum(grow - gc, 0.0))
            s = jnp.sum(qrow * k * decay, axis=-1, keepdims=True)
            s = jnp.where(row_id <= i, s, 0.0)
            intra_ref[pl.ds(i, 1), :] = jnp.sum(s * v, axis=0, keepdims=True)
            return carry

        lax.fori_loop(0, chunk, row, 0)
        intra = intra_ref[...]
    o_ref[rows, :] = _gla_gate_out(inter + intra, r_ref[rows, :], ng_ref[...]).astype(o_ref.dtype)
    kd = (k * jnp.exp(g_last - gc)).astype(BF16)
    return st * jnp.exp(g_last) + lax.dot_general(v.astype(BF16), kd, (((0,), (0,)), ((), ())),
                                                  preferred_element_type=F32)


def _gla_prompt_kernel(q_ref, k_ref, v_ref, g_ref, r_ref, ng_ref, tri_ref, o_ref, s_ref, st_ref, intra_ref,
                       *, chunk, n_chunks, dk):
    refs = (q_ref, k_ref, v_ref, g_ref, r_ref, ng_ref, o_ref)
    tri = tri_ref[...]
    chunk_decay = jnp.sum(g_ref[...].reshape(n_chunks, chunk, LANES), axis=1)
    factorised_ok = jnp.min(chunk_decay) >= -GLA_FACTORISED_MAX_DECAY

    @pl.when(factorised_ok)
    def _():
        st = jnp.zeros((LANES, LANES), F32)
        for c in range(n_chunks):
            st = _gla_chunk(refs, slice(c * chunk, (c + 1) * chunk), st, tri, exact=False)
        st_ref[...] = st

    @pl.when(jnp.logical_not(factorised_ok))
    def _():
        def body(c, st):
            rows = pl.ds(pl.multiple_of(c * chunk, chunk), chunk)
            return _gla_chunk(refs, rows, st, tri, exact=True, intra_ref=intra_ref)
        st_ref[...] = lax.fori_loop(0, n_chunks, body, jnp.zeros((LANES, LANES), F32))

    s_ref[...] = st_ref[...].T[:dk, :]


def _gla_prompt(gq, gk, gv, gg, gr, norm_g, *, batch, seq, heads, dk, chunk=128):
    m = gq.shape[0]
    n_chunks = seq // chunk
    tri = (lax.broadcasted_iota(jnp.int32, (chunk, chunk), 0)
           >= lax.broadcasted_iota(jnp.int32, (chunk, chunk), 1)).astype(F32)
    kern = functools.partial(_gla_prompt_kernel, chunk=chunk, n_chunks=n_chunks, dk=dk)
    blk = pl.BlockSpec((seq, LANES), lambda b, h: (b, h))
    return pl.pallas_call(
        kern,
        grid=(batch, heads),
        in_specs=[blk, blk, blk, blk, blk,
                  pl.BlockSpec((1, LANES), lambda b, h: (0, h)),
                  pl.BlockSpec((chunk, chunk), lambda b, h: (0, 0))],
        out_specs=[blk, pl.BlockSpec((None, None, dk, LANES), lambda b, h: (b, h, 0, 0))],
        out_shape=[jax.ShapeDtypeStruct((m, heads * LANES), BF16),
                   jax.ShapeDtypeStruct((batch, heads, dk, LANES), F32)],
        scratch_shapes=[pltpu.VMEM((LANES, LANES), F32), pltpu.VMEM((chunk, LANES), F32)],
        compiler_params=_params(("parallel", "parallel")),
        name="gla_prompt",
    )(gq, gk, gv, gg, gr, norm_g, tri)


def _mem_kv_kernel(x_ref, wk_ref, wv_ref, kf_ref, vf_ref, kb_ref, vb_ref):
    xb = x_ref[...].astype(BF16)
    k = jnp.dot(xb, wk_ref[...], preferred_element_type=F32)
    v = jnp.dot(xb, wv_ref[...], preferred_element_type=F32)
    kf_ref[...] = k
    vf_ref[...] = v
    kb_ref[...] = k.astype(BF16)
    vb_ref[...] = v.astype(BF16)


def _mem_kv(mem2d, wk, wv, *, tm=512):
    m, d = mem2d.shape
    w = wk.shape[1]
    row = lambda width: pl.BlockSpec((tm, width), lambda i: (i, 0))
    const = pl.BlockSpec((d, w), lambda i: (0, 0))
    return pl.pallas_call(
        _mem_kv_kernel,
        grid=(m // tm,),
        in_specs=[row(d), const, const],
        out_specs=[row(w)] * 4,
        out_shape=[jax.ShapeDtypeStruct((m, w), F32)] * 2 + [jax.ShapeDtypeStruct((m, w), BF16)] * 2,
        compiler_params=_params(("parallel",)),
        name="mem_kv",
    )(mem2d, wk, wv)


def _softmax_rows(s):
    s = s - jnp.max(s, axis=-1, keepdims=True)
    p = jnp.exp(s)
    return p / jnp.sum(p, axis=-1, keepdims=True)


def _wo_ln_kernel(*refs, alpha, sba_w, gvw, mem_heads, hd, fuse_mem):
    if fuse_mem:
        x_ref, a_ref, g_ref, mq_ref, mk_ref, mv_ref, wo_ref, lg_ref, lb_ref, o_ref = refs
    else:
        x_ref, a_ref, g_ref, om_ref, wo_ref, lg_ref, lb_ref, o_ref = refs
    acc = alpha * x_ref[...]
    acc = acc + jnp.dot(a_ref[...], wo_ref[0:sba_w, :], preferred_element_type=F32)
    acc = acc + jnp.dot(g_ref[...], wo_ref[sba_w:sba_w + gvw, :], preferred_element_type=F32)
    base = sba_w + gvw
    if fuse_mem:
        om = []
        for h in range(mem_heads):
            lanes = slice(h * hd, (h + 1) * hd)
            s = lax.dot_general(mq_ref[:, lanes], mk_ref[:, lanes], (((1,), (1,)), ((), ())),
                                preferred_element_type=F32)
            p = _softmax_rows(s)
            om.append(jnp.dot(p.astype(BF16), mv_ref[:, lanes], preferred_element_type=F32).astype(BF16))
        acc = acc + jnp.dot(jnp.concatenate(om, axis=1), wo_ref[base:, :], preferred_element_type=F32)
    else:
        acc = acc + jnp.dot(om_ref[...], wo_ref[base:, :], preferred_element_type=F32)
    o_ref[...] = _layer_norm(acc, lg_ref[...], lb_ref[...])


def _wo_ln(x2d, o_sba, og, mem_args, wo, ln_g, ln_b, *, tm, alpha, mem_heads, hd, rows_per_batch=None):
    m, d = x2d.shape
    sba_w, gvw = o_sba.shape[1], og.shape[1]
    fuse_mem = rows_per_batch is not None
    row = lambda w: pl.BlockSpec((tm, w), lambda i: (i, 0))
    const = lambda shape: pl.BlockSpec(shape, lambda i: (0, 0), pipeline_mode=pl.Buffered(1))
    if fuse_mem:
        mq, mk, mv = mem_args
        n_mem = mk.shape[0] // (m // rows_per_batch)
        tiles_per_batch = rows_per_batch // tm
        per_batch = pl.BlockSpec((n_mem, mk.shape[1]), lambda i: (i // tiles_per_batch, 0))
        mem_specs = [row(mq.shape[1]), per_batch, per_batch]
    else:
        mem_specs = [row(mem_args[0].shape[1])]
    kern = functools.partial(_wo_ln_kernel, alpha=alpha, sba_w=sba_w, gvw=gvw, mem_heads=mem_heads, hd=hd,
                             fuse_mem=fuse_mem)
    return pl.pallas_call(
        kern,
        grid=(m // tm,),
        in_specs=[row(d), row(sba_w), row(gvw)] + mem_specs + [const(wo.shape), const(ln_g.shape), const(ln_b.shape)],
        out_specs=row(d),
        out_shape=jax.ShapeDtypeStruct((m, d), F32),
        compiler_params=_params(("parallel",)),
        name="wo_ln1",
    )(x2d, o_sba, og, *mem_args, wo, ln_g, ln_b)


def _mlp_init(x_ref, xb_ref, acc_ref):
    xb_ref[...] = x_ref[...].astype(BF16)
    acc_ref[...] = jnp.zeros_like(acc_ref)


def _mlp_accumulate(xb_ref, wu_ref, wd_ref, acc_ref):
    _mlp_down(_mlp_up(xb_ref, wu_ref), wd_ref, acc_ref)


def _mlp_up(xb_ref, wu_ref):
    h = jnp.dot(xb_ref[...], wu_ref[...], preferred_element_type=F32)
    return jnp.square(jnp.maximum(h, 0.0)).astype(BF16)


def _mlp_down(u, wd_ref, acc_ref):
    acc_ref[...] += jnp.dot(u, wd_ref[...], preferred_element_type=F32)


def _mlp_finish(x_ref, acc_ref, lg_ref, lb_ref, o_ref, alpha):
    o_ref[...] = _layer_norm(alpha * x_ref[...] + acc_ref[...], lg_ref[...], lb_ref[...])


def _mlp_kernel(x_ref, wu_ref, wd_ref, lg_ref, lb_ref, o_ref, xb_ref, acc_ref, *, alpha):
    f = pl.program_id(1)

    @pl.when(f == 0)
    def _():
        _mlp_init(x_ref, xb_ref, acc_ref)

    _mlp_accumulate(xb_ref, wu_ref, wd_ref, acc_ref)

    @pl.when(f == pl.num_programs(1) - 1)
    def _():
        _mlp_finish(x_ref, acc_ref, lg_ref, lb_ref, o_ref, alpha)


def _mlp_ln(x1, wu, wd, ln_g, ln_b, *, tm, tf, alpha):
    m, d = x1.shape
    dff = wu.shape[1]
    kern = functools.partial(_mlp_kernel, alpha=alpha)
    return pl.pallas_call(
        kern,
        grid=(m // tm, dff // tf),
        in_specs=[pl.BlockSpec((tm, d), lambda i, f: (i, 0)),
                  pl.BlockSpec((d, tf), lambda i, f: (0, f)),
                  pl.BlockSpec((tf, d), lambda i, f: (f, 0)),
                  pl.BlockSpec((1, d), lambda i, f: (0, 0)),
                  pl.BlockSpec((1, d), lambda i, f: (0, 0))],
        out_specs=pl.BlockSpec((tm, d), lambda i, f: (i, 0)),
        out_shape=jax.ShapeDtypeStruct((m, d), F32),
        scratch_shapes=[pltpu.VMEM((tm, d), BF16), pltpu.VMEM((tm, d), F32)],
        compiler_params=_params(("parallel", "arbitrary")),
        name="mlp_ln2",
    )(x1, wu, wd, ln_g, ln_b)


def _decode_init(acc_ref, run_ref):
    acc_ref[...] = jnp.zeros_like(acc_ref)
    run_ref[...] = jnp.zeros_like(run_ref)


def _dot_2pass(a, b):
    hi = a.astype(BF16)
    lo = (a - hi.astype(F32)).astype(BF16)
    return jnp.dot(hi, b, preferred_element_type=F32) + jnp.dot(lo, b, preferred_element_type=F32)


def _decode_accumulate(bias_ref, q_ref, k_refs, v_refs, u_ref, newer_ref, acc_ref, run_ref, heads):
    z = _decode_scores(bias_ref, q_ref, k_refs, heads)
    _decode_values(z, v_refs, u_ref, newer_ref, acc_ref, run_ref, heads)


def _own_lanes(heads):
    lane_head = lax.broadcasted_iota(jnp.int32, (heads, LANES), 1) & (heads - 1)
    return lane_head == lax.broadcasted_iota(jnp.int32, (heads, LANES), 0)


def _decode_scores(bias_ref, q_ref, k_refs, heads):
    pages_per_step = len(k_refs)
    groups = k_refs[0].shape[0] // LANES
    own = _own_lanes(heads)
    q = q_ref[...].astype(F32)
    qt = lax.dot_general(q, own.astype(F32), (((0,), (0,)), ((), ())), preferred_element_type=F32)
    rows = []
    for j in range(pages_per_step):
        for g in reversed(range(groups)):
            kt = k_refs[j][g * LANES:(g + 1) * LANES, :].T
            rows.append(jnp.sum(kt * qt, axis=0, keepdims=True))
    return (jnp.concatenate(rows, axis=0) + bias_ref[...]) * LN2


def _decode_values(z, v_refs, u_ref, newer_ref, acc_ref, run_ref, heads):
    pages_per_step = len(v_refs)
    groups = v_refs[0].shape[0] // LANES
    own = _own_lanes(heads)
    sp = _softplus(z)
    log_keep = -sp
    log_beta = z - sp
    ext = _dot_2pass(log_keep, u_ref[...])
    incl, total = ext[:, :LANES], ext[:, LANES:]
    run = run_ref[...]
    tot_hi = total.astype(BF16)
    tot_lo = (total - tot_hi.astype(F32)).astype(BF16)
    newer = (jnp.dot(newer_ref[...], tot_hi, preferred_element_type=F32)
             + jnp.dot(newer_ref[...], tot_lo, preferred_element_type=F32))
    wgt = jnp.exp(log_beta + (incl - log_keep) + newer + run)
    run_ref[...] = run + jnp.sum(total, axis=0, keepdims=True)
    acc = acc_ref[...]
    for j in range(pages_per_step):
        first = j * groups
        wj = jnp.concatenate(
            [jnp.where(own, jnp.broadcast_to(wgt[first + groups - 1 - g:first + groups - g], own.shape), 0.0)
             for g in range(groups)], axis=1)
        acc = acc + jnp.dot(wj, v_refs[j][...], preferred_element_type=F32)
    acc_ref[...] = acc


def _sba_decode_kernel(pt_ref, bias_ref, q_ref, *refs, pages_per_step, heads):
    k_refs = refs[:pages_per_step]
    v_refs = refs[pages_per_step:2 * pages_per_step]
    u_ref, newer_ref, o_ref, acc_ref, run_ref = refs[2 * pages_per_step:]
    c = pl.program_id(1)

    @pl.when(c == 0)
    def _():
        _decode_init(acc_ref, run_ref)

    _decode_accumulate(bias_ref, q_ref, k_refs, v_refs, u_ref, newer_ref, acc_ref, run_ref, heads)

    @pl.when(c == pl.num_programs(1) - 1)
    def _():
        o_ref[...] = acc_ref[...]


def _mlp_decode_kernel(pt_ref, x_ref, wu_ref, wd_ref, lg_ref, lb_ref, bias_ref, q_ref, *refs,
                       alpha, pages_per_step, heads):
    k_refs = refs[:pages_per_step]
    v_refs = refs[pages_per_step:2 * pages_per_step]
    u_ref, newer_ref, o_ref, osba_ref, xb_ref, acc_ref, dacc_ref, drun_ref = refs[2 * pages_per_step:]
    f = pl.program_id(1)

    @pl.when(f == 0)
    def _():
        _mlp_init(x_ref, xb_ref, acc_ref)
        _decode_init(dacc_ref, drun_ref)

    u = _mlp_up(xb_ref, wu_ref)
    z = _decode_scores(bias_ref, q_ref, k_refs, heads)
    _mlp_down(u, wd_ref, acc_ref)
    _decode_values(z, v_refs, u_ref, newer_ref, dacc_ref, drun_ref, heads)

    @pl.when(f == pl.num_programs(1) - 1)
    def _():
        _mlp_finish(x_ref, acc_ref, lg_ref, lb_ref, o_ref, alpha)
        osba_ref[...] = dacc_ref[...]


def _decode_operands(q, cache_k, cache_v, layer, page_table, bias, pages_per_step):
    n_seq, n_pages = page_table.shape
    depth, n_phys, page, heads, hd = cache_k.shape
    assert heads & (heads - 1) == 0 and LANES % heads == 0 and (page * heads) % LANES == 0
    rows = page * heads
    ck = cache_k.reshape(depth * n_phys, rows, hd)
    cv = cache_v.reshape(depth * n_phys, rows, hd)
    src = lax.broadcasted_iota(jnp.int32, (LANES, LANES), 0)
    dst = lax.broadcasted_iota(jnp.int32, (LANES, LANES), 1)
    same_head = ((src - dst) & (heads - 1)) == 0
    u = jnp.concatenate([(same_head & (src >= dst)).astype(BF16), same_head.astype(BF16)], axis=1)
    n_blocks = pages_per_step * rows // LANES
    newer = (lax.broadcasted_iota(jnp.int32, (n_blocks, n_blocks), 1)
             < lax.broadcasted_iota(jnp.int32, (n_blocks, n_blocks), 0)).astype(BF16)
    bias_row = jnp.tile(bias.astype(F32), LANES // heads).reshape(1, LANES)

    def page_spec(j):
        def index(b, c, pt):
            logical = n_pages - 1 - (c * pages_per_step + j)
            return (layer * n_phys + pt[b * n_pages + logical], 0, 0)
        return pl.BlockSpec((None, rows, hd), index)

    const = lambda a: pl.BlockSpec(a.shape, lambda b, c, pt: (0, 0))
    seq_spec = pl.BlockSpec((None, heads, hd), lambda b, c, pt: (b, 0, 0))
    head_specs = [const(bias_row), seq_spec]
    page_specs = [page_spec(j) for j in range(pages_per_step)] * 2 + [const(u), const(newer)]
    head_args = (bias_row, q.reshape(n_seq, heads, hd))
    page_args = (*([ck] * pages_per_step), *([cv] * pages_per_step), u, newer)
    scratch = [pltpu.VMEM((heads, hd), F32), pltpu.VMEM((1, LANES), F32)]
    return head_specs, page_specs, head_args, page_args, seq_spec, scratch


def _sba_decode(q, cache_k, cache_v, layer, page_table, bias, *, pages_per_step=16):
    n_seq, n_pages = page_table.shape
    heads, hd = cache_k.shape[3:]
    head_specs, page_specs, head_args, page_args, seq_spec, scratch = _decode_operands(
        q, cache_k, cache_v, layer, page_table, bias, pages_per_step)
    kern = functools.partial(_sba_decode_kernel, pages_per_step=pages_per_step, heads=heads)
    grid_spec = pltpu.PrefetchScalarGridSpec(
        num_scalar_prefetch=1,
        grid=(n_seq, n_pages // pages_per_step),
        in_specs=head_specs + page_specs,
        out_specs=seq_spec,
        scratch_shapes=scratch,
    )
    out = pl.pallas_call(
        kern,
        grid_spec=grid_spec,
        out_shape=jax.ShapeDtypeStruct((n_seq, heads, hd), F32),
        compiler_params=_params(("parallel", "arbitrary")),
        name="sba_decode",
    )(page_table.reshape(-1), *head_args, *page_args)
    return out.reshape(n_seq, heads * hd)


SHARED_GRID_MAX_TM = 512
SHARED_GRID_TF = 512


def _shared_grid_tiles(m, dff, n_seq, n_pages):
    if m % n_seq or dff % SHARED_GRID_TF:
        return None, None
    tm, col_steps = m // n_seq, dff // SHARED_GRID_TF
    if tm % 8 or tm > SHARED_GRID_MAX_TM or n_pages % col_steps:
        return None, None
    return tm, SHARED_GRID_TF


def _mlp_ln_with_decode(x1, wu, wd, ln_g, ln_b, q, cache_k, cache_v, layer, page_table, bias, *, tm, tf, alpha):
    m, d = x1.shape
    dff = wu.shape[1]
    n_seq, n_pages = page_table.shape
    heads, hd = cache_k.shape[3:]
    grid = (m // tm, dff // tf)
    assert grid[0] == n_seq and n_pages % grid[1] == 0
    pages_per_step = n_pages // grid[1]
    head_specs, page_specs, head_args, page_args, seq_spec, scratch = _decode_operands(
        q, cache_k, cache_v, layer, page_table, bias, pages_per_step)
    kern = functools.partial(_mlp_decode_kernel, alpha=alpha, pages_per_step=pages_per_step, heads=heads)
    grid_spec = pltpu.PrefetchScalarGridSpec(
        num_scalar_prefetch=1,
        grid=grid,
        in_specs=[pl.BlockSpec((tm, d), lambda i, f, pt: (i, 0)),
                  pl.BlockSpec((d, tf), lambda i, f, pt: (0, f)),
                  pl.BlockSpec((tf, d), lambda i, f, pt: (f, 0)),
                  pl.BlockSpec((1, d), lambda i, f, pt: (0, 0)),
                  pl.BlockSpec((1, d), lambda i, f, pt: (0, 0))] + head_specs + page_specs,
        out_specs=[pl.BlockSpec((tm, d), lambda i, f, pt: (i, 0)), seq_spec],
        scratch_shapes=[pltpu.VMEM((tm, d), BF16), pltpu.VMEM((tm, d), F32)] + scratch,
    )
    y, o_sba = pl.pallas_call(
        kern,
        grid_spec=grid_spec,
        out_shape=[jax.ShapeDtypeStruct((m, d), F32), jax.ShapeDtypeStruct((n_seq, heads, hd), F32)],
        compiler_params=_params(("parallel", "arbitrary")),
        name="mlp_ln2_sba_decode",
    )(page_table.reshape(-1), x1, wu, wd, ln_g, ln_b, *head_args, *page_args)
    return y, o_sba.reshape(n_seq, heads * hd)


def _gla_step_kernel(q_ref, k_ref, g_ref, v_ref, r_ref, ng_ref, s_ref, o_ref, sn_ref):
    s_new = jnp.exp(g_ref[...]) * s_ref[...].astype(F32) + k_ref[...] * v_ref[...]
    sn_ref[...] = s_new
    o = jnp.sum(q_ref[...] * s_new, axis=2, keepdims=True)
    o_ref[...] = _gla_gate_out(o, r_ref[...], ng_ref[...]).astype(o_ref.dtype)


def _gla_step(q_col, k_col, g_col, v_row, r_row, norm_g, state, layer, *, bb=8):
    depth, n_seq, heads, dk, dv = state.shape
    steps = n_seq // bb
    col = pl.BlockSpec((bb, heads, dk, 1), lambda i: (i, 0, 0, 0))
    rowv = pl.BlockSpec((bb, heads, 1, dv), lambda i: (i, 0, 0, 0))
    st = pl.BlockSpec((bb, heads, dk, dv), lambda i: (i, 0, 0, 0))
    st_in = pl.BlockSpec((bb, heads, dk, dv), lambda i: (layer * steps + i, 0, 0, 0))
    return pl.pallas_call(
        _gla_step_kernel,
        grid=(steps,),
        in_specs=[col, col, col, rowv, rowv, pl.BlockSpec((1, heads, 1, dv), lambda i: (0, 0, 0, 0)), st_in],
        out_specs=[rowv, st],
        out_shape=[jax.ShapeDtypeStruct((n_seq, heads, 1, dv), BF16),
                   jax.ShapeDtypeStruct((n_seq, heads, dk, dv), F32)],
        compiler_params=_params(("parallel",)),
        name="gla_step",
    )(q_col, k_col, g_col, v_row, r_row, norm_g, state.reshape(depth * n_seq, heads, dk, dv))


def _mem_decode_kernel(q_ref, k_ref, v_ref, o_ref, *, heads, seqs):
    rows = k_ref.shape[1]
    own = ((lax.broadcasted_iota(jnp.int32, (heads, rows), 1) & (heads - 1))
           == lax.broadcasted_iota(jnp.int32, (heads, rows), 0))
    for b in range(seqs):
        s = lax.dot_general(q_ref[b].astype(F32), k_ref[b], (((1,), (1,)), ((), ())), preferred_element_type=F32)
        s = jnp.where(own, s, -1e30)
        p = jnp.where(own, jnp.exp(s - jnp.max(s, axis=-1, keepdims=True)), 0.0)
        o = jnp.dot(p, v_ref[b], preferred_element_type=F32)
        o_ref[b] = (o / jnp.sum(p, axis=-1, keepdims=True)).astype(o_ref.dtype)


def _mem_decode(mq, mem_k, mem_v, layer, *, seqs_per_step=8):
    depth, n_seq, n_mem, heads, hd = mem_k.shape
    assert heads & (heads - 1) == 0
    rows = n_mem * heads
    steps = n_seq // seqs_per_step
    kern = functools.partial(_mem_decode_kernel, heads=heads, seqs=seqs_per_step)
    vec = pl.BlockSpec((seqs_per_step, heads, hd), lambda i: (i, 0, 0))
    kv = pl.BlockSpec((seqs_per_step, rows, hd), lambda i: (layer * steps + i, 0, 0))
    out = pl.pallas_call(
        kern,
        grid=(steps,),
        in_specs=[vec, kv, kv],
        out_specs=vec,
        out_shape=jax.ShapeDtypeStruct((n_seq, heads, hd), BF16),
        compiler_params=_params(("parallel",)),
        name="mem_decode",
    )(mq.reshape(n_seq, heads, hd), mem_k.reshape(depth * n_seq, rows, hd), mem_v.reshape(depth * n_seq, rows, hd))
    return out.reshape(n_seq, heads * hd)


def kernel(x_prompt, x_sample, cache_sba_k, cache_sba_v, state_gla, cache_mem_k, cache_mem_v, page_table,
           mem_prompt, w_in, sba_bias, w_gate_up, b_gate, gla_norm_g, w_mem_k, w_mem_v, w_o, ln1_g, ln1_b,
           w_up, w_down, ln2_g, ln2_b):
    depth = w_in.shape[0]
    bp, tp, d = x_prompt.shape
    bs, ts, _ = x_sample.shape
    assert ts == 1, "the decode path handles one new token per sequence"
    _, n_phys, page, sba_h, sba_d = cache_sba_k.shape
    _, _, gla_h, gla_k, gla_v = state_gla.shape
    _, _, n_mem, mem_h, mem_d = cache_mem_k.shape
    rank = w_gate_up.shape[1]
    assert gla_v == LANES and gla_k <= LANES and sba_d == LANES and mem_d == LANES
    sba_w, gkw, gvw, mem_w = sba_h * sba_d, gla_h * gla_k, gla_h * gla_v, mem_h * mem_d
    gp_w = gla_h * LANES
    alpha = (2.0 * depth) ** 0.25
    dims = (sba_w, gkw, gvw, rank, mem_w, gla_h, gla_k)
    proj = functools.partial(_project, sba_w=sba_w, gp_w=gp_w, gvw=gvw, mem_w=mem_w,
                             sba_d=sba_d, gla_k=gla_k, mem_d=mem_d)

    yp = x_prompt.reshape(bp * tp, d)
    ys = x_sample.reshape(bs * ts, d)
    outs = [[] for _ in range(8)]
    for l in range(depth):
        w_packed, wg, bg = _pack_w_in(w_in[l], w_gate_up[l], b_gate[l], dims)
        wo = w_o[l].astype(BF16)
        wu = w_up[l].astype(BF16)
        wd = w_down[l].astype(BF16)
        ng = gla_norm_g[l].reshape(1, gvw)
        l1g, l1b = ln1_g[l].reshape(1, d), ln1_b[l].reshape(1, d)
        l2g, l2b = ln2_g[l].reshape(1, d), ln2_b[l].reshape(1, d)
        bias = sba_bias[l].astype(F32)

        q, kf, vf, kb, vb, gq, gk, gv, gr, gg, mq = proj(yp, w_packed, wg, bg, tm=256)
        o_sba = _sba_prompt(q, kb, vb, bias * LOG2E, batch=bp, seq=tp, heads=sba_h, hd=sba_d)
        og, s_p = _gla_prompt(gq, gk, gv, gg, gr, ng, batch=bp, seq=tp, heads=gla_h, dk=gla_k)
        mkf, mvf, mkb, mvb = _mem_kv(mem_prompt.reshape(bp * n_mem, d), w_mem_k[l].astype(BF16),
                                     w_mem_v[l].astype(BF16))
        x1 = _wo_ln(yp, o_sba, og, (mq, mkb, mvb), wo, l1g, l1b, tm=512, alpha=alpha, mem_heads=mem_h, hd=mem_d,
                    rows_per_batch=tp)
        outs[0].append(kf.reshape(bp, tp, sba_h, sba_d))
        outs[1].append(vf.reshape(bp, tp, sba_h, sba_d))
        outs[2].append(s_p)
        outs[3].append(mkf.reshape(bp, n_mem, mem_h, mem_d))
        outs[4].append(mvf.reshape(bp, n_mem, mem_h, mem_d))

        q2, kf2, vf2, _, _, gq2, gk2, gv2, gr2, gg2, mq2 = proj(ys, w_packed, wg, bg, tm=bs)
        decode_args = (q2, cache_sba_k, cache_sba_v, l, page_table, bias * LOG2E)
        tm_shared, tf_shared = _shared_grid_tiles(bp * tp, wu.shape[1], *page_table.shape)
        if tm_shared is not None:
            yp_next, o_sba2 = _mlp_ln_with_decode(x1, wu, wd, l2g, l2b, *decode_args, tm=tm_shared, tf=tf_shared,
                                                  alpha=alpha)
        else:
            yp_next = _mlp_ln(x1, wu, wd, l2g, l2b, tm=512, tf=1024, alpha=alpha)
            o_sba2 = _sba_decode(*decode_args)
        o_sba2 = o_sba2.astype(BF16)
        col = lambda a: a.reshape(bs, gla_h, LANES)[:, :, :gla_k].reshape(bs, gla_h, gla_k, 1)
        rowv = lambda a: a.reshape(bs, gla_h, 1, gla_v)
        og2, s_s = _gla_step(col(gq2), col(gk2), col(gg2), rowv(gv2), rowv(gr2), ng.reshape(1, gla_h, 1, gla_v),
                             state_gla, l)
        o_mem2 = _mem_decode(mq2, cache_mem_k, cache_mem_v, l)
        x1s = _wo_ln(ys, o_sba2, og2.reshape(bs, gvw), (o_mem2,), wo, l1g, l1b, tm=bs, alpha=alpha,
                     mem_heads=mem_h, hd=mem_d)
        ys_next = _mlp_ln(x1s, wu, wd, l2g, l2b, tm=bs, tf=1024, alpha=alpha)
        outs[5].append(kf2.reshape(bs, ts, sba_h, sba_d))
        outs[6].append(vf2.reshape(bs, ts, sba_h, sba_d))
        outs[7].append(s_s)

        yp, ys = yp_next, ys_next

    return (yp.reshape(bp, tp, d), ys.reshape(bs, ts, d), *[jnp.stack(o) for o in outs])
```

```python
import functools
import math

import jax
import jax.numpy as jnp
from jax import lax
from jax.experimental import pallas as pl
from jax.experimental.pallas import tpu as pltpu

F32 = jnp.float32
BF16 = jnp.bfloat16

LANES = 128
LN_EPS = 1e-5
NORM_EPS = 1e-6
GLA_TAU = 16.0
LOG2E = math.log2(math.e)
LN2 = math.log(2.0)
VMEM_LIMIT_BYTES = 56 * 1024 * 1024


def _params(sem):
    return pltpu.CompilerParams(dimension_semantics=sem, vmem_limit_bytes=VMEM_LIMIT_BYTES)


def _softplus(z):
    return jnp.maximum(z, 0.0) + jnp.log1p(jnp.exp(-jnp.abs(z)))


def _layer_norm(v, g, b):
    mu = jnp.mean(v, axis=-1, keepdims=True)
    vc = v - mu
    var = jnp.mean(vc * vc, axis=-1, keepdims=True)
    return vc * lax.rsqrt(var + LN_EPS) * g + b


def _pack_w_in(w_in, w_gate_up, b_gate, dims):
    sba_w, gkw, gvw, rank, mem_w, gh, gk = dims
    o = [0]
    for s in (sba_w, sba_w, sba_w, gkw, gkw, gvw, gvw, rank, mem_w):
        o.append(o[-1] + s)
    d = w_in.shape[0]

    def pad_heads(w):
        w = w.reshape(d, gh, gk)
        return jnp.pad(w, ((0, 0), (0, 0), (0, LANES - gk))).reshape(d, gh * LANES)

    cols = [w_in[:, o[0]:o[3]], pad_heads(w_in[:, o[3]:o[4]]), pad_heads(w_in[:, o[4]:o[5]]),
            w_in[:, o[5]:o[7]], w_in[:, o[8]:o[9]],
            jnp.pad(w_in[:, o[7]:o[8]], ((0, 0), (0, LANES - rank)))]
    w_packed = jnp.concatenate(cols, axis=1).astype(BF16)
    wg = jnp.pad(w_gate_up.reshape(rank, gh, gk), ((0, LANES - rank), (0, 0), (0, LANES - gk)))
    wg = wg.reshape(LANES, gh * LANES).astype(BF16)
    bg = jnp.pad(b_gate.reshape(1, gh, gk), ((0, 0), (0, 0), (0, LANES - gk))).reshape(1, gh * LANES)
    return w_packed, wg, bg.astype(F32)


def _proj_kernel(x_ref, w_ref, wg_ref, bg_ref,
                 q_ref, kf_ref, vf_ref, kb_ref, vb_ref, gq_ref, gk_ref, gv_ref, gr_ref, gg_ref, mq_ref,
                 *, sba_w, gp_w, gvw, mem_w, sba_scale, gla_scale, mem_scale):
    xb = x_ref[...].astype(BF16)

    def mm(lo, width):
        return jnp.dot(xb, w_ref[:, lo:lo + width], preferred_element_type=F32)

    c = 0
    q_ref[...] = (mm(c, sba_w) * sba_scale).astype(BF16); c += sba_w
    k = mm(c, sba_w); c += sba_w
    kf_ref[...] = k
    kb_ref[...] = k.astype(BF16)
    v = mm(c, sba_w); c += sba_w
    vf_ref[...] = v
    vb_ref[...] = v.astype(BF16)
    gq_ref[...] = mm(c, gp_w) * gla_scale; c += gp_w
    gk_ref[...] = mm(c, gp_w); c += gp_w
    gv_ref[...] = mm(c, gvw); c += gvw
    gr_ref[...] = mm(c, gvw); c += gvw
    mq_ref[...] = (mm(c, mem_w) * mem_scale).astype(BF16); c += mem_w
    glow = mm(c, LANES)
    pre = jnp.dot(glow.astype(BF16), wg_ref[...], preferred_element_type=F32) + bg_ref[...]
    gg_ref[...] = -_softplus(-pre) * (1.0 / GLA_TAU)


def _project(x2d, w_packed, wg, bg, *, tm, sba_w, gp_w, gvw, mem_w, sba_d, gla_k, mem_d):
    m, d = x2d.shape
    npk = w_packed.shape[1]
    kern = functools.partial(_proj_kernel, sba_w=sba_w, gp_w=gp_w, gvw=gvw, mem_w=mem_w,
                             sba_scale=sba_d ** -0.5 * LOG2E, gla_scale=gla_k ** -0.5, mem_scale=mem_d ** -0.5)
    row = lambda w: pl.BlockSpec((tm, w), lambda i: (i, 0))
    const = lambda shape: pl.BlockSpec(shape, lambda i: (0, 0), pipeline_mode=pl.Buffered(1))
    out_shapes = [(sba_w, BF16), (sba_w, F32), (sba_w, F32), (sba_w, BF16), (sba_w, BF16),
                  (gp_w, F32), (gp_w, F32), (gvw, F32), (gvw, F32), (gp_w, F32), (mem_w, BF16)]
    return pl.pallas_call(
        kern,
        grid=(m // tm,),
        in_specs=[row(d), const((d, npk)), const(wg.shape), const(bg.shape)],
        out_specs=[row(w) for w, _ in out_shapes],
        out_shape=[jax.ShapeDtypeStruct((m, w), dt) for w, dt in out_shapes],
        compiler_params=_params(("parallel",)),
        name="in_proj",
    )(x2d, w_packed, wg, bg)


SBA_HEADS_PER_SUFFIX_DOT = 4


def _sba_tile(q_ref, k_ref, v_ref, u_ref, acc_ref, run_ref, biases, heads, start, *, tq, hd, diag):
    lanes = [slice(h * hd, (h + 1) * hd) for h in heads]
    sps, log_betas = [], []
    for h, ln in zip(heads, lanes):
        k = k_ref[pl.ds(start, tq), ln]
        z = lax.dot_general(q_ref[:, ln], k, (((1,), (1,)), ((), ())), preferred_element_type=F32) + biases[h]
        sp = jnp.maximum(z, 0.0) + jnp.log2(1.0 + jnp.exp2(-jnp.abs(z)))
        log_beta = z - sp
        if diag:
            keep = lax.broadcasted_iota(jnp.int32, (tq, tq), 1) < lax.broadcasted_iota(jnp.int32, (tq, tq), 0)
            sp = jnp.where(keep, sp, 0.0)
            log_beta = jnp.where(keep, log_beta, -1e30)
        sps.append(sp.astype(BF16))
        log_betas.append(log_beta)
    ext_all = jnp.dot(jnp.concatenate(sps, axis=0), u_ref[...], preferred_element_type=F32)
    for n, (h, ln) in enumerate(zip(heads, lanes)):
        ext = ext_all[n * tq:(n + 1) * tq]
        rest, total = ext[:, :tq], ext[:, tq:]
        v = v_ref[pl.ds(start, tq), ln]
        if diag:
            w = jnp.exp2(log_betas[n] - rest)
            acc_ref[:, ln] = jnp.dot(w.astype(BF16), v, preferred_element_type=F32)
            run_ref[h] = total
        else:
            run = run_ref[h]
            w = jnp.exp2(log_betas[n] - rest - jnp.concatenate([run] * (tq // LANES), axis=1))
            acc_ref[:, ln] += jnp.dot(w.astype(BF16), v, preferred_element_type=F32)
            run_ref[h] = run + total


def _sba_prompt_kernel(bias_ref, q_ref, k_ref, v_ref, u_ref, o_ref, acc_ref, run_ref, *, tq, hd, heads_per_step):
    i = pl.program_id(2)
    hg = pl.program_id(1)
    tile = functools.partial(_sba_tile, q_ref, k_ref, v_ref, u_ref, acc_ref, run_ref, tq=tq, hd=hd)
    biases = [bias_ref[hg * heads_per_step + h] for h in range(heads_per_step)]

    groups = [tuple(range(g, min(g + SBA_HEADS_PER_SUFFIX_DOT, heads_per_step)))
              for g in range(0, heads_per_step, SBA_HEADS_PER_SUFFIX_DOT)]
    for heads in groups:
        tile(biases, heads, pl.multiple_of(i * tq, tq), diag=True)

    def body(jj, carry):
        start = pl.multiple_of((i - 1 - jj) * tq, tq)
        for heads in groups:
            tile(biases, heads, start, diag=False)
        return carry

    lax.fori_loop(0, i, body, 0)
    o_ref[...] = acc_ref[...].astype(o_ref.dtype)


def _sba_prompt(q, k, v, bias, *, batch, seq, heads, hd, tq=256, heads_per_step=8):
    m = q.shape[0]
    nq = seq // tq
    w = heads_per_step * hd
    strict = lax.broadcasted_iota(jnp.int32, (tq, tq), 0) > lax.broadcasted_iota(jnp.int32, (tq, tq), 1)
    u = jnp.concatenate([strict.astype(BF16), jnp.ones((tq, LANES), BF16)], axis=1)
    kern = functools.partial(_sba_prompt_kernel, tq=tq, hd=hd, heads_per_step=heads_per_step)
    return pl.pallas_call(
        kern,
        grid=(batch, heads // heads_per_step, nq),
        in_specs=[pl.BlockSpec(memory_space=pltpu.SMEM),
                  pl.BlockSpec((tq, w), lambda b, h, i: (b * nq + i, h)),
                  pl.BlockSpec((seq, w), lambda b, h, i: (b, h)),
                  pl.BlockSpec((seq, w), lambda b, h, i: (b, h)),
                  pl.BlockSpec(u.shape, lambda b, h, i: (0, 0))],
        out_specs=pl.BlockSpec((tq, w), lambda b, h, i: (b * nq + i, h)),
        out_shape=jax.ShapeDtypeStruct((m, heads * hd), BF16),
        scratch_shapes=[pltpu.VMEM((tq, w), F32), pltpu.VMEM((heads_per_step, tq, LANES), F32)],
        compiler_params=_params(("parallel", "parallel", "arbitrary")),
        name="sba_prompt",
    )(bias, q, k, v, u)


def _gla_gate_out(o, r, g):
    og = o * lax.rsqrt(jnp.mean(o * o, axis=-1, keepdims=True) + NORM_EPS)
    return og * g * (r * jax.nn.sigmoid(r))


GLA_FACTORISED_MAX_DECAY = 80.0


def _gla_chunk(refs, rows, st, tri, *, exact, intra_ref=None):
    q_ref, k_ref, v_ref, g_ref, r_ref, ng_ref, o_ref = refs
    chunk = tri.shape[0]
    q = q_ref[rows, :]
    k = k_ref[rows, :]
    v = v_ref[rows, :]
    gc = jnp.dot(tri, g_ref[rows, :], preferred_element_type=F32, precision=lax.Precision.HIGHEST)
    g_last = gc[chunk - 1:chunk, :]
    inter = lax.dot_general((q * jnp.exp(gc)).astype(BF16), st.astype(BF16),
                            (((1,), (1,)), ((), ())), preferred_element_type=F32)
    if not exact:
        g_mid = gc[chunk // 2 - 1:chunk // 2, :]
        qi = (q * jnp.exp(gc - g_mid)).astype(BF16)
        ki = (k * jnp.exp(g_mid - gc)).astype(BF16)
        scores = lax.dot_general(qi, ki, (((1,), (1,)), ((), ())), preferred_element_type=F32)
        scores = jnp.where(tri > 0.5, scores, 0.0)
        intra = jnp.dot(scores.astype(BF16), v.astype(BF16), preferred_element_type=F32)
    else:
        row_id = lax.broadcasted_iota(jnp.int32, (chunk, 1), 0)

        def row(i, carry):
            sel = row_id == i
            qrow = jnp.sum(jnp.where(sel, q, 0.0), axis=0, keepdims=True)
            grow = jnp.sum(jnp.where(sel, gc, 0.0), axis=0, keepdims=True)
            decay = jnp.exp(jnp.minimum(grow - gc, 0.0))
            s = jnp.sum(qrow * k * decay, axis=-1, keepdims=True)
            s = jnp.where(row_id <= i, s, 0.0)
            intra_ref[pl.ds(i, 1), :] = jnp.sum(s * v, axis=0, keepdims=True)
            return carry

        lax.fori_loop(0, chunk, row, 0)
        intra = intra_ref[...]
    o_ref[rows, :] = _gla_gate_out(inter + intra, r_ref[rows, :], ng_ref[...]).astype(o_ref.dtype)
    kd = (k * jnp.exp(g_last - gc)).astype(BF16)
    return st * jnp.exp(g_last) + lax.dot_general(v.astype(BF16), kd, (((0,), (0,)), ((), ())),
                                                  preferred_element_type=F32)


def _gla_prompt_kernel(q_ref, k_ref, v_ref, g_ref, r_ref, ng_ref, tri_ref, o_ref, s_ref, st_ref, intra_ref,
                       *, chunk, n_chunks, dk):
    refs = (q_ref, k_ref, v_ref, g_ref, r_ref, ng_ref, o_ref)
    tri = tri_ref[...]
    chunk_decay = jnp.sum(g_ref[...].reshape(n_chunks, chunk, LANES), axis=1)
    factorised_ok = jnp.min(chunk_decay) >= -GLA_FACTORISED_MAX_DECAY

    @pl.when(factorised_ok)
    def _():
        st = jnp.zeros((LANES, LANES), F32)
        for c in range(n_chunks):
            st = _gla_chunk(refs, slice(c * chunk, (c + 1) * chunk), st, tri, exact=False)
        st_ref[...] = st

    @pl.when(jnp.logical_not(factorised_ok))
    def _():
        def body(c, st):
            rows = pl.ds(pl.multiple_of(c * chunk, chunk), chunk)
            return _gla_chunk(refs, rows, st, tri, exact=True, intra_ref=intra_ref)
        st_ref[...] = lax.fori_loop(0, n_chunks, body, jnp.zeros((LANES, LANES), F32))

    s_ref[...] = st_ref[...].T[:dk, :]


def _gla_prompt(gq, gk, gv, gg, gr, norm_g, *, batch, seq, heads, dk, chunk=128):
    m = gq.shape[0]
    n_chunks = seq // chunk
    tri = (lax.broadcasted_iota(jnp.int32, (chunk, chunk), 0)
           >= lax.broadcasted_iota(jnp.int32, (chunk, chunk), 1)).astype(F32)
    kern = functools.partial(_gla_prompt_kernel, chunk=chunk, n_chunks=n_chunks, dk=dk)
    blk = pl.BlockSpec((seq, LANES), lambda b, h: (b, h))
    return pl.pallas_call(
        kern,
        grid=(batch, heads),
        in_specs=[blk, blk, blk, blk, blk,
                  pl.BlockSpec((1, LANES), lambda b, h: (0, h)),
                  pl.BlockSpec((chunk, chunk), lambda b, h: (0, 0))],
        out_specs=[blk, pl.BlockSpec((None, None, dk, LANES), lambda b, h: (b, h, 0, 0))],
        out_shape=[jax.ShapeDtypeStruct((m, heads * LANES), BF16),
                   jax.ShapeDtypeStruct((batch, heads, dk, LANES), F32)],
        scratch_shapes=[pltpu.VMEM((LANES, LANES), F32), pltpu.VMEM((chunk, LANES), F32)],
        compiler_params=_params(("parallel", "parallel")),
        name="gla_prompt",
    )(gq, gk, gv, gg, gr, norm_g, tri)


def _mem_kv_kernel(x_ref, wk_ref, wv_ref, kf_ref, vf_ref, kb_ref, vb_ref):
    xb = x_ref[...].astype(BF16)
    k = jnp.dot(xb, wk_ref[...], preferred_element_type=F32)
    v = jnp.dot(xb, wv_ref[...], preferred_element_type=F32)
    kf_ref[...] = k
    vf_ref[...] = v
    kb_ref[...] = k.astype(BF16)
    vb_ref[...] = v.astype(BF16)


def _mem_kv(mem2d, wk, wv, *, tm=512):
    m, d = mem2d.shape
    w = wk.shape[1]
    row = lambda width: pl.BlockSpec((tm, width), lambda i: (i, 0))
    const = pl.BlockSpec((d, w), lambda i: (0, 0))
    return pl.pallas_call(
        _mem_kv_kernel,
        grid=(m // tm,),
        in_specs=[row(d), const, const],
        out_specs=[row(w)] * 4,
        out_shape=[jax.ShapeDtypeStruct((m, w), F32)] * 2 + [jax.ShapeDtypeStruct((m, w), BF16)] * 2,
        compiler_params=_params(("parallel",)),
        name="mem_kv",
    )(mem2d, wk, wv)


def _softmax_rows(s):
    s = s - jnp.max(s, axis=-1, keepdims=True)
    p = jnp.exp(s)
    return p / jnp.sum(p, axis=-1, keepdims=True)


def _wo_ln_kernel(*refs, alpha, sba_w, gvw, mem_heads, hd, fuse_mem):
    if fuse_mem:
        x_ref, a_ref, g_ref, mq_ref, mk_ref, mv_ref, wo_ref, lg_ref, lb_ref, o_ref = refs
    else:
        x_ref, a_ref, g_ref, om_ref, wo_ref, lg_ref, lb_ref, o_ref = refs
    acc = alpha * x_ref[...]
    acc = acc + jnp.dot(a_ref[...], wo_ref[0:sba_w, :], preferred_element_type=F32)
    acc = acc + jnp.dot(g_ref[...], wo_ref[sba_w:sba_w + gvw, :], preferred_element_type=F32)
    base = sba_w + gvw
    if fuse_mem:
        om = []
        for h in range(mem_heads):
            lanes = slice(h * hd, (h + 1) * hd)
            s = lax.dot_general(mq_ref[:, lanes], mk_ref[:, lanes], (((1,), (1,)), ((), ())),
                                preferred_element_type=F32)
            p = _softmax_rows(s)
            om.append(jnp.dot(p.astype(BF16), mv_ref[:, lanes], preferred_element_type=F32).astype(BF16))
        acc = acc + jnp.dot(jnp.concatenate(om, axis=1), wo_ref[base:, :], preferred_element_type=F32)
    else:
        acc = acc + jnp.dot(om_ref[...], wo_ref[base:, :], preferred_element_type=F32)
    o_ref[...] = _layer_norm(acc, lg_ref[...], lb_ref[...])


def _wo_ln(x2d, o_sba, og, mem_args, wo, ln_g, ln_b, *, tm, alpha, mem_heads, hd, rows_per_batch=None):
    m, d = x2d.shape
    sba_w, gvw = o_sba.shape[1], og.shape[1]
    fuse_mem = rows_per_batch is not None
    row = lambda w: pl.BlockSpec((tm, w), lambda i: (i, 0))
    const = lambda shape: pl.BlockSpec(shape, lambda i: (0, 0), pipeline_mode=pl.Buffered(1))
    if fuse_mem:
        mq, mk, mv = mem_args
        n_mem = mk.shape[0] // (m // rows_per_batch)
        tiles_per_batch = rows_per_batch // tm
        per_batch = pl.BlockSpec((n_mem, mk.shape[1]), lambda i: (i // tiles_per_batch, 0))
        mem_specs = [row(mq.shape[1]), per_batch, per_batch]
    else:
        mem_specs = [row(mem_args[0].shape[1])]
    kern = functools.partial(_wo_ln_kernel, alpha=alpha, sba_w=sba_w, gvw=gvw, mem_heads=mem_heads, hd=hd,
                             fuse_mem=fuse_mem)
    return pl.pallas_call(
        kern,
        grid=(m // tm,),
        in_specs=[row(d), row(sba_w), row(gvw)] + mem_specs + [const(wo.shape), const(ln_g.shape), const(ln_b.shape)],
        out_specs=row(d),
        out_shape=jax.ShapeDtypeStruct((m, d), F32),
        compiler_params=_params(("parallel",)),
        name="wo_ln1",
    )(x2d, o_sba, og, *mem_args, wo, ln_g, ln_b)


def _mlp_init(x_ref, xb_ref, acc_ref):
    xb_ref[...] = x_ref[...].astype(BF16)
    acc_ref[...] = jnp.zeros_like(acc_ref)


def _mlp_accumulate(xb_ref, wu_ref, wd_ref, acc_ref):
    _mlp_down(_mlp_up(xb_ref, wu_ref), wd_ref, acc_ref)


def _mlp_up(xb_ref, wu_ref):
    h = jnp.dot(xb_ref[...], wu_ref[...], preferred_element_type=F32)
    return jnp.square(jnp.maximum(h, 0.0)).astype(BF16)


def _mlp_down(u, wd_ref, acc_ref):
    acc_ref[...] += jnp.dot(u, wd_ref[...], preferred_element_type=F32)


def _mlp_finish(x_ref, acc_ref, lg_ref, lb_ref, o_ref, alpha):
    o_ref[...] = _layer_norm(alpha * x_ref[...] + acc_ref[...], lg_ref[...], lb_ref[...])


def _mlp_kernel(x_ref, wu_ref, wd_ref, lg_ref, lb_ref, o_ref, xb_ref, acc_ref, *, alpha):
    f = pl.program_id(1)

    @pl.when(f == 0)
    def _():
        _mlp_init(x_ref, xb_ref, acc_ref)

    _mlp_accumulate(xb_ref, wu_ref, wd_ref, acc_ref)

    @pl.when(f == pl.num_programs(1) - 1)
    def _():
        _mlp_finish(x_ref, acc_ref, lg_ref, lb_ref, o_ref, alpha)


def _mlp_ln(x1, wu, wd, ln_g, ln_b, *, tm, tf, alpha):
    m, d = x1.shape
    dff = wu.shape[1]
    kern = functools.partial(_mlp_kernel, alpha=alpha)
    return pl.pallas_call(
        kern,
        grid=(m // tm, dff // tf),
        in_specs=[pl.BlockSpec((tm, d), lambda i, f: (i, 0)),
                  pl.BlockSpec((d, tf), lambda i, f: (0, f)),
                  pl.BlockSpec((tf, d), lambda i, f: (f, 0)),
                  pl.BlockSpec((1, d), lambda i, f: (0, 0)),
                  pl.BlockSpec((1, d), lambda i, f: (0, 0))],
        out_specs=pl.BlockSpec((tm, d), lambda i, f: (i, 0)),
        out_shape=jax.ShapeDtypeStruct((m, d), F32),
        scratch_shapes=[pltpu.VMEM((tm, d), BF16), pltpu.VMEM((tm, d), F32)],
        compiler_params=_params(("parallel", "arbitrary")),
        name="mlp_ln2",
    )(x1, wu, wd, ln_g, ln_b)


def _decode_init(acc_ref, run_ref):
    acc_ref[...] = jnp.zeros_like(acc_ref)
    run_ref[...] = jnp.zeros_like(run_ref)


def _dot_2pass(a, b):
    hi = a.astype(BF16)
    lo = (a - hi.astype(F32)).astype(BF16)
    return jnp.dot(hi, b, preferred_element_type=F32) + jnp.dot(lo, b, preferred_element_type=F32)


def _decode_accumulate(bias_ref, q_ref, k_refs, v_refs, u_ref, newer_ref, acc_ref, run_ref, heads):
    z = _decode_scores(bias_ref, q_ref, k_refs, heads)
    _decode_values(z, v_refs, u_ref, newer_ref, acc_ref, run_ref, heads)


def _own_lanes(heads):
    lane_head = lax.broadcasted_iota(jnp.int32, (heads, LANES), 1) & (heads - 1)
    return lane_head == lax.broadcasted_iota(jnp.int32, (heads, LANES), 0)


def _decode_scores(bias_ref, q_ref, k_refs, heads):
    pages_per_step = len(k_refs)
    groups = k_refs[0].shape[0] // LANES
    own = _own_lanes(heads)
    q = q_ref[...].astype(F32)
    qt = lax.dot_general(q, own.astype(F32), (((0,), (0,)), ((), ())), preferred_element_type=F32)
    rows = []
    for j in range(pages_per_step):
        for g in reversed(range(groups)):
            kt = k_refs[j][g * LANES:(g + 1) * LANES, :].T
            rows.append(jnp.sum(kt * qt, axis=0, keepdims=True))
    return (jnp.concatenate(rows, axis=0) + bias_ref[...]) * LN2


def _decode_values(z, v_refs, u_ref, newer_ref, acc_ref, run_ref, heads):
    pages_per_step = len(v_refs)
    groups = v_refs[0].shape[0] // LANES
    own = _own_lanes(heads)
    sp = _softplus(z)
    log_keep = -sp
    log_beta = z - sp
    ext = _dot_2pass(log_keep, u_ref[...])
    incl, total = ext[:, :LANES], ext[:, LANES:]
    run = run_ref[...]
    tot_hi = total.astype(BF16)
    tot_lo = (total - tot_hi.astype(F32)).astype(BF16)
    newer = (jnp.dot(newer_ref[...], tot_hi, preferred_element_type=F32)
             + jnp.dot(newer_ref[...], tot_lo, preferred_element_type=F32))
    wgt = jnp.exp(log_beta + (incl - log_keep) + newer + run)
    run_ref[...] = run + jnp.sum(total, axis=0, keepdims=True)
    acc = acc_ref[...]
    for j in range(pages_per_step):
        first = j * groups
        wj = jnp.concatenate(
            [jnp.where(own, jnp.broadcast_to(wgt[first + groups - 1 - g:first + groups - g], own.shape), 0.0)
             for g in range(groups)], axis=1)
        acc = acc + jnp.dot(wj, v_refs[j][...], preferred_element_type=F32)
    acc_ref[...] = acc


def _sba_decode_kernel(pt_ref, bias_ref, q_ref, *refs, pages_per_step, heads):
    k_refs = refs[:pages_per_step]
    v_refs = refs[pages_per_step:2 * pages_per_step]
    u_ref, newer_ref, o_ref, acc_ref, run_ref = refs[2 * pages_per_step:]
    c = pl.program_id(1)

    @pl.when(c == 0)
    def _():
        _decode_init(acc_ref, run_ref)

    _decode_accumulate(bias_ref, q_ref, k_refs, v_refs, u_ref, newer_ref, acc_ref, run_ref, heads)

    @pl.when(c == pl.num_programs(1) - 1)
    def _():
        o_ref[...] = acc_ref[...]


def _mlp_decode_kernel(pt_ref, x_ref, wu_ref, wd_ref, lg_ref, lb_ref, bias_ref, q_ref, *refs,
                       alpha, pages_per_step, heads):
    k_refs = refs[:pages_per_step]
    v_refs = refs[pages_per_step:2 * pages_per_step]
    u_ref, newer_ref, o_ref, osba_ref, xb_ref, acc_ref, dacc_ref, drun_ref = refs[2 * pages_per_step:]
    f = pl.program_id(1)

    @pl.when(f == 0)
    def _():
        _mlp_init(x_ref, xb_ref, acc_ref)
        _decode_init(dacc_ref, drun_ref)

    u = _mlp_up(xb_ref, wu_ref)
    z = _decode_scores(bias_ref, q_ref, k_refs, heads)
    _mlp_down(u, wd_ref, acc_ref)
    _decode_values(z, v_refs, u_ref, newer_ref, dacc_ref, drun_ref, heads)

    @pl.when(f == pl.num_programs(1) - 1)
    def _():
        _mlp_finish(x_ref, acc_ref, lg_ref, lb_ref, o_ref, alpha)
        osba_ref[...] = dacc_ref[...]


def _decode_operands(q, cache_k, cache_v, layer, page_table, bias, pages_per_step):
    n_seq, n_pages = page_table.shape
    depth, n_phys, page, heads, hd = cache_k.shape
    assert heads & (heads - 1) == 0 and LANES % heads == 0 and (page * heads) % LANES == 0
    rows = page * heads
    ck = cache_k.reshape(depth * n_phys, rows, hd)
    cv = cache_v.reshape(depth * n_phys, rows, hd)
    src = lax.broadcasted_iota(jnp.int32, (LANES, LANES), 0)
    dst = lax.broadcasted_iota(jnp.int32, (LANES, LANES), 1)
    same_head = ((src - dst) & (heads - 1)) == 0
    u = jnp.concatenate([(same_head & (src >= dst)).astype(BF16), same_head.astype(BF16)], axis=1)
    n_blocks = pages_per_step * rows // LANES
    newer = (lax.broadcasted_iota(jnp.int32, (n_blocks, n_blocks), 1)
             < lax.broadcasted_iota(jnp.int32, (n_blocks, n_blocks), 0)).astype(BF16)
    bias_row = jnp.tile(bias.astype(F32), LANES // heads).reshape(1, LANES)

    def page_spec(j):
        def index(b, c, pt):
            logical = n_pages - 1 - (c * pages_per_step + j)
            return (layer * n_phys + pt[b * n_pages + logical], 0, 0)
        return pl.BlockSpec((None, rows, hd), index)

    const = lambda a: pl.BlockSpec(a.shape, lambda b, c, pt: (0, 0))
    seq_spec = pl.BlockSpec((None, heads, hd), lambda b, c, pt: (b, 0, 0))
    head_specs = [const(bias_row), seq_spec]
    page_specs = [page_spec(j) for j in range(pages_per_step)] * 2 + [const(u), const(newer)]
    head_args = (bias_row, q.reshape(n_seq, heads, hd))
    page_args = (*([ck] * pages_per_step), *([cv] * pages_per_step), u, newer)
    scratch = [pltpu.VMEM((heads, hd), F32), pltpu.VMEM((1, LANES), F32)]
    return head_specs, page_specs, head_args, page_args, seq_spec, scratch


def _sba_decode(q, cache_k, cache_v, layer, page_table, bias, *, pages_per_step=16):
    n_seq, n_pages = page_table.shape
    heads, hd = cache_k.shape[3:]
    head_specs, page_specs, head_args, page_args, seq_spec, scratch = _decode_operands(
        q, cache_k, cache_v, layer, page_table, bias, pages_per_step)
    kern = functools.partial(_sba_decode_kernel, pages_per_step=pages_per_step, heads=heads)
    grid_spec = pltpu.PrefetchScalarGridSpec(
        num_scalar_prefetch=1,
        grid=(n_seq, n_pages // pages_per_step),
        in_specs=head_specs + page_specs,
        out_specs=seq_spec,
        scratch_shapes=scratch,
    )
    out = pl.pallas_call(
        kern,
        grid_spec=grid_spec,
        out_shape=jax.ShapeDtypeStruct((n_seq, heads, hd), F32),
        compiler_params=_params(("parallel", "arbitrary")),
        name="sba_decode",
    )(page_table.reshape(-1), *head_args, *page_args)
    return out.reshape(n_seq, heads * hd)


SHARED_GRID_MAX_TM = 512
SHARED_GRID_TF = 512


def _shared_grid_tiles(m, dff, n_seq, n_pages):
    if m % n_seq or dff % SHARED_GRID_TF:
        return None, None
    tm, col_steps = m // n_seq, dff // SHARED_GRID_TF
    if tm % 8 or tm > SHARED_GRID_MAX_TM or n_pages % col_steps:
        return None, None
    return tm, SHARED_GRID_TF


def _mlp_ln_with_decode(x1, wu, wd, ln_g, ln_b, q, cache_k, cache_v, layer, page_table, bias, *, tm, tf, alpha):
    m, d = x1.shape
    dff = wu.shape[1]
    n_seq, n_pages = page_table.shape
    heads, hd = cache_k.shape[3:]
    grid = (m // tm, dff // tf)
    assert grid[0] == n_seq and n_pages % grid[1] == 0
    pages_per_step = n_pages // grid[1]
    head_specs, page_specs, head_args, page_args, seq_spec, scratch = _decode_operands(
        q, cache_k, cache_v, layer, page_table, bias, pages_per_step)
    kern = functools.partial(_mlp_decode_kernel, alpha=alpha, pages_per_step=pages_per_step, heads=heads)
    grid_spec = pltpu.PrefetchScalarGridSpec(
        num_scalar_prefetch=1,
        grid=grid,
        in_specs=[pl.BlockSpec((tm, d), lambda i, f, pt: (i, 0)),
                  pl.BlockSpec((d, tf), lambda i, f, pt: (0, f)),
                  pl.BlockSpec((tf, d), lambda i, f, pt: (f, 0)),
                  pl.BlockSpec((1, d), lambda i, f, pt: (0, 0)),
                  pl.BlockSpec((1, d), lambda i, f, pt: (0, 0))] + head_specs + page_specs,
        out_specs=[pl.BlockSpec((tm, d), lambda i, f, pt: (i, 0)), seq_spec],
        scratch_shapes=[pltpu.VMEM((tm, d), BF16), pltpu.VMEM((tm, d), F32)] + scratch,
    )
    y, o_sba = pl.pallas_call(
        kern,
        grid_spec=grid_spec,
        out_shape=[jax.ShapeDtypeStruct((m, d), F32), jax.ShapeDtypeStruct((n_seq, heads, hd), F32)],
        compiler_params=_params(("parallel", "arbitrary")),
        name="mlp_ln2_sba_decode",
    )(page_table.reshape(-1), x1, wu, wd, ln_g, ln_b, *head_args, *page_args)
    return y, o_sba.reshape(n_seq, heads * hd)


def _gla_step_kernel(q_ref, k_ref, g_ref, v_ref, r_ref, ng_ref, s_ref, o_ref, sn_ref):
    s_new = jnp.exp(g_ref[...]) * s_ref[...].astype(F32) + k_ref[...] * v_ref[...]
    sn_ref[...] = s_new
    o = jnp.sum(q_ref[...] * s_new, axis=2, keepdims=True)
    o_ref[...] = _gla_gate_out(o, r_ref[...], ng_ref[...]).astype(o_ref.dtype)


def _gla_step(q_col, k_col, g_col, v_row, r_row, norm_g, state, layer, *, bb=8):
    depth, n_seq, heads, dk, dv = state.shape
    steps = n_seq // bb
    col = pl.BlockSpec((bb, heads, dk, 1), lambda i: (i, 0, 0, 0))
    rowv = pl.BlockSpec((bb, heads, 1, dv), lambda i: (i, 0, 0, 0))
    st = pl.BlockSpec((bb, heads, dk, dv), lambda i: (i, 0, 0, 0))
    st_in = pl.BlockSpec((bb, heads, dk, dv), lambda i: (layer * steps + i, 0, 0, 0))
    return pl.pallas_call(
        _gla_step_kernel,
        grid=(steps,),
        in_specs=[col, col, col, rowv, rowv, pl.BlockSpec((1, heads, 1, dv), lambda i: (0, 0, 0, 0)), st_in],
        out_specs=[rowv, st],
        out_shape=[jax.ShapeDtypeStruct((n_seq, heads, 1, dv), BF16),
                   jax.ShapeDtypeStruct((n_seq, heads, dk, dv), F32)],
        compiler_params=_params(("parallel",)),
        name="gla_step",
    )(q_col, k_col, g_col, v_row, r_row, norm_g, state.reshape(depth * n_seq, heads, dk, dv))


def _mem_decode_kernel(q_ref, k_ref, v_ref, o_ref, *, heads, seqs):
    rows = k_ref.shape[1]
    own = ((lax.broadcasted_iota(jnp.int32, (heads, rows), 1) & (heads - 1))
           == lax.broadcasted_iota(jnp.int32, (heads, rows), 0))
    for b in range(seqs):
        s = lax.dot_general(q_ref[b].astype(F32), k_ref[b], (((1,), (1,)), ((), ())), preferred_element_type=F32)
        s = jnp.where(own, s, -1e30)
        p = jnp.where(own, jnp.exp(s - jnp.max(s, axis=-1, keepdims=True)), 0.0)
        o = jnp.dot(p, v_ref[b], preferred_element_type=F32)
        o_ref[b] = (o / jnp.sum(p, axis=-1, keepdims=True)).astype(o_ref.dtype)


def _mem_decode(mq, mem_k, mem_v, layer, *, seqs_per_step=8):
    depth, n_seq, n_mem, heads, hd = mem_k.shape
    assert heads & (heads - 1) == 0
    rows = n_mem * heads
    steps = n_seq // seqs_per_step
    kern = functools.partial(_mem_decode_kernel, heads=heads, seqs=seqs_per_step)
    vec = pl.BlockSpec((seqs_per_step, heads, hd), lambda i: (i, 0, 0))
    kv = pl.BlockSpec((seqs_per_step, rows, hd), lambda i: (layer * steps + i, 0, 0))
    out = pl.pallas_call(
        kern,
        grid=(steps,),
        in_specs=[vec, kv, kv],
        out_specs=vec,
        out_shape=jax.ShapeDtypeStruct((n_seq, heads, hd), BF16),
        compiler_params=_params(("parallel",)),
        name="mem_decode",
    )(mq.reshape(n_seq, heads, hd), mem_k.reshape(depth * n_seq, rows, hd), mem_v.reshape(depth * n_seq, rows, hd))
    return out.reshape(n_seq, heads * hd)


def kernel(x_prompt, x_sample, cache_sba_k, cache_sba_v, state_gla, cache_mem_k, cache_mem_v, page_table,
           mem_prompt, w_in, sba_bias, w_gate_up, b_gate, gla_norm_g, w_mem_k, w_mem_v, w_o, ln1_g, ln1_b,
           w_up, w_down, ln2_g, ln2_b):
    depth = w_in.shape[0]
    bp, tp, d = x_prompt.shape
    bs, ts, _ = x_sample.shape
    assert ts == 1, "the decode path handles one new token per sequence"
    _, n_phys, page, sba_h, sba_d = cache_sba_k.shape
    _, _, gla_h, gla_k, gla_v = state_gla.shape
    _, _, n_mem, mem_h, mem_d = cache_mem_k.shape
    rank = w_gate_up.shape[1]
    assert gla_v == LANES and gla_k <= LANES and sba_d == LANES and mem_d == LANES
    sba_w, gkw, gvw, mem_w = sba_h * sba_d, gla_h * gla_k, gla_h * gla_v, mem_h * mem_d
    gp_w = gla_h * LANES
    alpha = (2.0 * depth) ** 0.25
    dims = (sba_w, gkw, gvw, rank, mem_w, gla_h, gla_k)
    proj = functools.partial(_project, sba_w=sba_w, gp_w=gp_w, gvw=gvw, mem_w=mem_w,
                             sba_d=sba_d, gla_k=gla_k, mem_d=mem_d)

    yp = x_prompt.reshape(bp * tp, d)
    ys = x_sample.reshape(bs * ts, d)
    outs = [[] for _ in range(8)]
    for l in range(depth):
        w_packed, wg, bg = _pack_w_in(w_in[l], w_gate_up[l], b_gate[l], dims)
        wo = w_o[l].astype(BF16)
        wu = w_up[l].astype(BF16)
        wd = w_down[l].astype(BF16)
        ng = gla_norm_g[l].reshape(1, gvw)
        l1g, l1b = ln1_g[l].reshape(1, d), ln1_b[l].reshape(1, d)
        l2g, l2b = ln2_g[l].reshape(1, d), ln2_b[l].reshape(1, d)
        bias = sba_bias[l].astype(F32)

        q, kf, vf, kb, vb, gq, gk, gv, gr, gg, mq = proj(yp, w_packed, wg, bg, tm=256)
        o_sba = _sba_prompt(q, kb, vb, bias * LOG2E, batch=bp, seq=tp, heads=sba_h, hd=sba_d)
        og, s_p = _gla_prompt(gq, gk, gv, gg, gr, ng, batch=bp, seq=tp, heads=gla_h, dk=gla_k)
        mkf, mvf, mkb, mvb = _mem_kv(mem_prompt.reshape(bp * n_mem, d), w_mem_k[l].astype(BF16),
                                     w_mem_v[l].astype(BF16))
        x1 = _wo_ln(yp, o_sba, og, (mq, mkb, mvb), wo, l1g, l1b, tm=512, alpha=alpha, mem_heads=mem_h, hd=mem_d,
                    rows_per_batch=tp)
        outs[0].append(kf.reshape(bp, tp, sba_h, sba_d))
        outs[1].append(vf.reshape(bp, tp, sba_h, sba_d))
        outs[2].append(s_p)
        outs[3].append(mkf.reshape(bp, n_mem, mem_h, mem_d))
        outs[4].append(mvf.reshape(bp, n_mem, mem_h, mem_d))

        q2, kf2, vf2, _, _, gq2, gk2, gv2, gr2, gg2, mq2 = proj(ys, w_packed, wg, bg, tm=bs)
        decode_args = (q2, cache_sba_k, cache_sba_v, l, page_table, bias * LOG2E)
        tm_shared, tf_shared = _shared_grid_tiles(bp * tp, wu.shape[1], *page_table.shape)
        if tm_shared is not None:
            yp_next, o_sba2 = _mlp_ln_with_decode(x1, wu, wd, l2g, l2b, *decode_args, tm=tm_shared, tf=tf_shared,
                                                  alpha=alpha)
        else:
            yp_next = _mlp_ln(x1, wu, wd, l2g, l2b, tm=512, tf=1024, alpha=alpha)
            o_sba2 = _sba_decode(*decode_args)
        o_sba2 = o_sba2.astype(BF16)
        col = lambda a: a.reshape(bs, gla_h, LANES)[:, :, :gla_k].reshape(bs, gla_h, gla_k, 1)
        rowv = lambda a: a.reshape(bs, gla_h, 1, gla_v)
        og2, s_s = _gla_step(col(gq2), col(gk2), col(gg2), rowv(gv2), rowv(gr2), ng.reshape(1, gla_h, 1, gla_v),
                             state_gla, l)
        o_mem2 = _mem_decode(mq2, cache_mem_k, cache_mem_v, l)
        x1s = _wo_ln(ys, o_sba2, og2.reshape(bs, gvw), (o_mem2,), wo, l1g, l1b, tm=bs, alpha=alpha,
                     mem_heads=mem_h, hd=mem_d)
        ys_next = _mlp_ln(x1s, wu, wd, l2g, l2b, tm=bs, tf=1024, alpha=alpha)
        outs[5].append(kf2.reshape(bs, ts, sba_h, sba_d))
        outs[6].append(vf2.reshape(bs, ts, sba_h, sba_d))
        outs[7].append(s_s)

        yp, ys = yp_next, ys_next

    return (yp.reshape(bp, tp, d), ys.reshape(bs, ts, d), *[jnp.stack(o) for o in outs])
```

```python
import functools
import math

import jax
import jax.numpy as jnp
from jax import lax
from jax.experimental import pallas as pl
from jax.experimental.pallas import tpu as pltpu

F32 = jnp.float32
BF16 = jnp.bfloat16

LANES = 128
BF16_SUBLANES = 16
LN_EPS = 1e-5
NORM_EPS = 1e-6
GLA_TAU = 16.0
LOG2E = math.log2(math.e)
LN2 = math.log(2.0)
VMEM_LIMIT_BYTES = 56 * 1024 * 1024


def _params(sem):
    return pltpu.CompilerParams(dimension_semantics=sem, vmem_limit_bytes=VMEM_LIMIT_BYTES)


def _softplus(z):
    return jnp.maximum(z, 0.0) + jnp.log1p(jnp.exp(-jnp.abs(z)))


def _layer_norm(v, g, b):
    mu = jnp.mean(v, axis=-1, keepdims=True)
    vc = v - mu
    var = jnp.mean(vc * vc, axis=-1, keepdims=True)
    return vc * lax.rsqrt(var + LN_EPS) * g + b


def _pack_w_in(w_in, w_gate_up, b_gate, dims):
    sba_w, gkw, gvw, rank, mem_w, gh, gk = dims
    o = [0]
    for s in (sba_w, sba_w, sba_w, gkw, gkw, gvw, gvw, rank, mem_w):
        o.append(o[-1] + s)
    d = w_in.shape[0]

    def pad_heads(w):
        w = w.reshape(d, gh, gk)
        return jnp.pad(w, ((0, 0), (0, 0), (0, LANES - gk))).reshape(d, gh * LANES)

    cols = [w_in[:, o[0]:o[3]], pad_heads(w_in[:, o[3]:o[4]]), pad_heads(w_in[:, o[4]:o[5]]),
            w_in[:, o[5]:o[7]], w_in[:, o[8]:o[9]],
            jnp.pad(w_in[:, o[7]:o[8]], ((0, 0), (0, LANES - rank)))]
    w_packed = jnp.concatenate(cols, axis=1).astype(BF16)
    wg = jnp.pad(w_gate_up.reshape(rank, gh, gk), ((0, LANES - rank), (0, 0), (0, LANES - gk)))
    wg = wg.reshape(LANES, gh * LANES).astype(BF16)
    bg = jnp.pad(b_gate.reshape(1, gh, gk), ((0, 0), (0, 0), (0, LANES - gk))).reshape(1, gh * LANES)
    return w_packed, wg, bg.astype(F32)


def _proj_kernel(x_ref, w_ref, wg_ref, bg_ref, *refs,
                 sba_w, gp_w, gvw, mem_w, sba_scale, gla_scale, mem_scale, n_cast):
    cast_in, refs = refs[:n_cast], refs[n_cast:]
    q_ref, kf_ref, vf_ref, kb_ref, vb_ref, gq_ref, gk_ref, gv_ref, gr_ref, gg_ref, mq_ref = refs[:11]
    for src, dst in zip(cast_in, refs[11:]):
        dst[...] = src[...].astype(dst.dtype)
    xb = x_ref[...].astype(BF16)

    def mm(lo, width):
        return jnp.dot(xb, w_ref[:, lo:lo + width], preferred_element_type=F32)

    c = 0
    q_ref[...] = (mm(c, sba_w) * sba_scale).astype(BF16); c += sba_w
    k = mm(c, sba_w); c += sba_w
    kf_ref[...] = k
    kb_ref[...] = k.astype(BF16)
    v = mm(c, sba_w); c += sba_w
    vf_ref[...] = v
    vb_ref[...] = v.astype(BF16)
    gq_ref[...] = mm(c, gp_w) * gla_scale; c += gp_w
    gk_ref[...] = mm(c, gp_w); c += gp_w
    gv_ref[...] = mm(c, gvw); c += gvw
    gr_ref[...] = mm(c, gvw); c += gvw
    mq_ref[...] = (mm(c, mem_w) * mem_scale).astype(BF16); c += mem_w
    glow = mm(c, LANES)
    pre = jnp.dot(glow.astype(BF16), wg_ref[...], preferred_element_type=F32) + bg_ref[...]
    gg_ref[...] = -_softplus(-pre) * (1.0 / GLA_TAU)


def _cast_slabs(weights, steps):
    specs = []
    for w in weights:
        axis = 0 if w.shape[0] >= w.shape[1] else 1
        size, rem = divmod(w.shape[axis], steps)
        if rem or size % (BF16_SUBLANES if axis == 0 else LANES):
            return None
        if axis == 0:
            specs.append(pl.BlockSpec((size, w.shape[1]), lambda i: (i, 0)))
        else:
            specs.append(pl.BlockSpec((w.shape[0], size), lambda i: (0, i)))
    return specs


def _project(x2d, w_packed, wg, bg, *, tm, sba_w, gp_w, gvw, mem_w, sba_d, gla_k, mem_d, cast_weights=()):
    m, d = x2d.shape
    npk = w_packed.shape[1]
    steps = m // tm
    cast_specs = _cast_slabs(cast_weights, steps) if cast_weights else []
    if cast_specs is None:
        outs = _project(x2d, w_packed, wg, bg, tm=tm, sba_w=sba_w, gp_w=gp_w, gvw=gvw, mem_w=mem_w,
                        sba_d=sba_d, gla_k=gla_k, mem_d=mem_d)
        return [*outs, *[w.astype(BF16) for w in cast_weights]]
    kern = functools.partial(_proj_kernel, sba_w=sba_w, gp_w=gp_w, gvw=gvw, mem_w=mem_w, n_cast=len(cast_specs),
                             sba_scale=sba_d ** -0.5 * LOG2E, gla_scale=gla_k ** -0.5, mem_scale=mem_d ** -0.5)
    row = lambda w: pl.BlockSpec((tm, w), lambda i: (i, 0))
    const = lambda shape: pl.BlockSpec(shape, lambda i: (0, 0), pipeline_mode=pl.Buffered(1))
    out_shapes = [(sba_w, BF16), (sba_w, F32), (sba_w, F32), (sba_w, BF16), (sba_w, BF16),
                  (gp_w, F32), (gp_w, F32), (gvw, F32), (gvw, F32), (gp_w, F32), (mem_w, BF16)]
    return pl.pallas_call(
        kern,
        grid=(steps,),
        in_specs=[row(d), const((d, npk)), const(wg.shape), const(bg.shape)] + cast_specs,
        out_specs=[row(w) for w, _ in out_shapes] + cast_specs,
        out_shape=[jax.ShapeDtypeStruct((m, w), dt) for w, dt in out_shapes]
                  + [jax.ShapeDtypeStruct(w.shape, BF16) for w in cast_weights],
        compiler_params=_params(("parallel",)),
        name="in_proj",
    )(x2d, w_packed, wg, bg, *cast_weights)


SBA_HEADS_PER_SUFFIX_DOT = 4


def _sba_tile(q_ref, k_ref, v_ref, u_ref, acc_ref, run_ref, biases, heads, start, *, tq, hd, diag):
    lanes = [slice(h * hd, (h + 1) * hd) for h in heads]
    sps, log_betas = [], []
    for h, ln in zip(heads, lanes):
        k = k_ref[pl.ds(start, tq), ln]
        z = lax.dot_general(q_ref[:, ln], k, (((1,), (1,)), ((), ())), preferred_element_type=F32) + biases[h]
        sp = jnp.maximum(z, 0.0) + jnp.log2(1.0 + jnp.exp2(-jnp.abs(z)))
        log_beta = z - sp
        if diag:
            keep = lax.broadcasted_iota(jnp.int32, (tq, tq), 1) < lax.broadcasted_iota(jnp.int32, (tq, tq), 0)
            sp = jnp.where(keep, sp, 0.0)
            log_beta = jnp.where(keep, log_beta, -1e30)
        sps.append(sp.astype(BF16))
        log_betas.append(log_beta)
    ext_all = jnp.dot(jnp.concatenate(sps, axis=0), u_ref[...], preferred_element_type=F32)
    for n, (h, ln) in enumerate(zip(heads, lanes)):
        ext = ext_all[n * tq:(n + 1) * tq]
        rest, total = ext[:, :tq], ext[:, tq:]
        v = v_ref[pl.ds(start, tq), ln]
        if diag:
            w = jnp.exp2(log_betas[n] - rest)
            acc_ref[:, ln] = jnp.dot(w.astype(BF16), v, preferred_element_type=F32)
            run_ref[h] = total
        else:
            run = run_ref[h]
            w = jnp.exp2(log_betas[n] - rest - jnp.concatenate([run] * (tq // LANES), axis=1))
            acc_ref[:, ln] += jnp.dot(w.astype(BF16), v, preferred_element_type=F32)
            run_ref[h] = run + total


def _sba_prompt_kernel(bias_ref, q_ref, k_ref, v_ref, u_ref, o_ref, acc_ref, run_ref, *, tq, hd, heads_per_step):
    i = pl.program_id(2)
    hg = pl.program_id(1)
    tile = functools.partial(_sba_tile, q_ref, k_ref, v_ref, u_ref, acc_ref, run_ref, tq=tq, hd=hd)
    biases = [bias_ref[hg * heads_per_step + h] for h in range(heads_per_step)]

    groups = [tuple(range(g, min(g + SBA_HEADS_PER_SUFFIX_DOT, heads_per_step)))
              for g in range(0, heads_per_step, SBA_HEADS_PER_SUFFIX_DOT)]
    for heads in groups:
        tile(biases, heads, pl.multiple_of(i * tq, tq), diag=True)

    def body(jj, carry):
        start = pl.multiple_of((i - 1 - jj) * tq, tq)
        for heads in groups:
            tile(biases, heads, start, diag=False)
        return carry

    lax.fori_loop(0, i, body, 0)
    o_ref[...] = acc_ref[...].astype(o_ref.dtype)


def _sba_prompt(q, k, v, bias, *, batch, seq, heads, hd, tq=256, heads_per_step=8):
    m = q.shape[0]
    nq = seq // tq
    w = heads_per_step * hd
    strict = lax.broadcasted_iota(jnp.int32, (tq, tq), 0) > lax.broadcasted_iota(jnp.int32, (tq, tq), 1)
    u = jnp.concatenate([strict.astype(BF16), jnp.ones((tq, LANES), BF16)], axis=1)
    kern = functools.partial(_sba_prompt_kernel, tq=tq, hd=hd, heads_per_step=heads_per_step)
    return pl.pallas_call(
        kern,
        grid=(batch, heads // heads_per_step, nq),
        in_specs=[pl.BlockSpec(memory_space=pltpu.SMEM),
                  pl.BlockSpec((tq, w), lambda b, h, i: (b * nq + i, h)),
                  pl.BlockSpec((seq, w), lambda b, h, i: (b, h)),
                  pl.BlockSpec((seq, w), lambda b, h, i: (b, h)),
                  pl.BlockSpec(u.shape, lambda b, h, i: (0, 0))],
        out_specs=pl.BlockSpec((tq, w), lambda b, h, i: (b * nq + i, h)),
        out_shape=jax.ShapeDtypeStruct((m, heads * hd), BF16),
        scratch_shapes=[pltpu.VMEM((tq, w), F32), pltpu.VMEM((heads_per_step, tq, LANES), F32)],
        compiler_params=_params(("parallel", "parallel", "arbitrary")),
        name="sba_prompt",
    )(bias, q, k, v, u)


def _gla_gate_out(o, r, g):
    og = o * lax.rsqrt(jnp.mean(o * o, axis=-1, keepdims=True) + NORM_EPS)
    return og * g * (r * jax.nn.sigmoid(r))


GLA_FACTORISED_MAX_DECAY = 80.0


def _gla_chunk(refs, rows, st, tri, *, exact, intra_ref=None):
    q_ref, k_ref, v_ref, g_ref, r_ref, ng_ref, o_ref = refs
    chunk = tri.shape[0]
    q = q_ref[rows, :]
    k = k_ref[rows, :]
    v = v_ref[rows, :]
    gc = jnp.dot(tri, g_ref[rows, :], preferred_element_type=F32, precision=lax.Precision.HIGHEST)
    g_last = gc[chunk - 1:chunk, :]
    inter = lax.dot_general((q * jnp.exp(gc)).astype(BF16), st.astype(BF16),
                            (((1,), (1,)), ((), ())), preferred_element_type=F32)
    if not exact:
        g_mid = gc[chunk // 2 - 1:chunk // 2, :]
        qi = (q * jnp.exp(gc - g_mid)).astype(BF16)
        ki = (k * jnp.exp(g_mid - gc)).astype(BF16)
        scores = lax.dot_general(qi, ki, (((1,), (1,)), ((), ())), preferred_element_type=F32)
        scores = jnp.where(tri > 0.5, scores, 0.0)
        intra = jnp.dot(scores.astype(BF16), v.astype(BF16), preferred_element_type=F32)
    else:
        row_id = lax.broadcasted_iota(jnp.int32, (chunk, 1), 0)

        def row(i, carry):
            sel = row_id == i
            qrow = jnp.sum(jnp.where(sel, q, 0.0), axis=0, keepdims=True)
            grow = jnp.sum(jnp.where(sel, gc, 0.0), axis=0, keepdims=True)
            decay = jnp.exp(jnp.minimum(grow - gc, 0.0))
            s = jnp.sum(qrow * k * decay, axis=-1, keepdims=True)
            s = jnp.where(row_id <= i, s, 0.0)
            intra_ref[pl.ds(i, 1), :] = jnp.sum(s * v, axis=0, keepdims=True)
            return carry

        lax.fori_loop(0, chunk, row, 0)
        intra = intra_ref[...]
    o_ref[rows, :] = _gla_gate_out(inter + intra, r_ref[rows, :], ng_ref[...]).astype(o_ref.dtype)
    kd = (k * jnp.exp(g_last - gc)).astype(BF16)
    return st * jnp.exp(g_last) + lax.dot_general(v.astype(BF16), kd, (((0,), (0,)), ((), ())),
                                                  preferred_element_type=F32)


def _gla_prompt_kernel(q_ref, k_ref, v_ref, g_ref, r_ref, ng_ref, tri_ref, o_ref, s_ref, st_ref, intra_ref,
                       *, chunk, n_chunks, dk):
    refs = (q_ref, k_ref, v_ref, g_ref, r_ref, ng_ref, o_ref)
    tri = tri_ref[...]
    chunk_decay = jnp.sum(g_ref[...].reshape(n_chunks, chunk, LANES), axis=1)
    factorised_ok = jnp.min(chunk_decay) >= -GLA_FACTORISED_MAX_DECAY

    @pl.when(factorised_ok)
    def _():
        st = jnp.zeros((LANES, LANES), F32)
        for c in range(n_chunks):
            st = _gla_chunk(refs, slice(c * chunk, (c + 1) * chunk), st, tri, exact=False)
        st_ref[...] = st

    @pl.when(jnp.logical_not(factorised_ok))
    def _():
        def body(c, st):
            rows = pl.ds(pl.multiple_of(c * chunk, chunk), chunk)
            return _gla_chunk(refs, rows, st, tri, exact=True, intra_ref=intra_ref)
        st_ref[...] = lax.fori_loop(0, n_chunks, body, jnp.zeros((LANES, LANES), F32))

    s_ref[...] = st_ref[...].T[:dk, :]


def _gla_prompt(gq, gk, gv, gg, gr, norm_g, *, batch, seq, heads, dk, chunk=128):
    m = gq.shape[0]
    n_chunks = seq // chunk
    tri = (lax.broadcasted_iota(jnp.int32, (chunk, chunk), 0)
           >= lax.broadcasted_iota(jnp.int32, (chunk, chunk), 1)).astype(F32)
    kern = functools.partial(_gla_prompt_kernel, chunk=chunk, n_chunks=n_chunks, dk=dk)
    blk = pl.BlockSpec((seq, LANES), lambda b, h: (b, h))
    return pl.pallas_call(
        kern,
        grid=(batch, heads),
        in_specs=[blk, blk, blk, blk, blk,
                  pl.BlockSpec((1, LANES), lambda b, h: (0, h)),
                  pl.BlockSpec((chunk, chunk), lambda b, h: (0, 0))],
        out_specs=[blk, pl.BlockSpec((None, None, dk, LANES), lambda b, h: (b, h, 0, 0))],
        out_shape=[jax.ShapeDtypeStruct((m, heads * LANES), BF16),
                   jax.ShapeDtypeStruct((batch, heads, dk, LANES), F32)],
        scratch_shapes=[pltpu.VMEM((LANES, LANES), F32), pltpu.VMEM((chunk, LANES), F32)],
        compiler_params=_params(("parallel", "parallel")),
        name="gla_prompt",
    )(gq, gk, gv, gg, gr, norm_g, tri)


def _mem_kv_kernel(x_ref, wk_ref, wv_ref, kf_ref, vf_ref, kb_ref, vb_ref):
    xb = x_ref[...].astype(BF16)
    k = jnp.dot(xb, wk_ref[...], preferred_element_type=F32)
    v = jnp.dot(xb, wv_ref[...], preferred_element_type=F32)
    kf_ref[...] = k
    vf_ref[...] = v
    kb_ref[...] = k.astype(BF16)
    vb_ref[...] = v.astype(BF16)


def _mem_kv(mem2d, wk, wv, *, tm=512):
    m, d = mem2d.shape
    w = wk.shape[1]
    row = lambda width: pl.BlockSpec((tm, width), lambda i: (i, 0))
    const = pl.BlockSpec((d, w), lambda i: (0, 0))
    return pl.pallas_call(
        _mem_kv_kernel,
        grid=(m // tm,),
        in_specs=[row(d), const, const],
        out_specs=[row(w)] * 4,
        out_shape=[jax.ShapeDtypeStruct((m, w), F32)] * 2 + [jax.ShapeDtypeStruct((m, w), BF16)] * 2,
        compiler_params=_params(("parallel",)),
        name="mem_kv",
    )(mem2d, wk, wv)


def _softmax_rows(s):
    s = s - jnp.max(s, axis=-1, keepdims=True)
    p = jnp.exp(s)
    return p / jnp.sum(p, axis=-1, keepdims=True)


def _wo_ln_kernel(*refs, alpha, sba_w, gvw, mem_heads, hd, fuse_mem):
    if fuse_mem:
        x_ref, a_ref, g_ref, mq_ref, mk_ref, mv_ref, wo_ref, lg_ref, lb_ref, o_ref = refs
    else:
        x_ref, a_ref, g_ref, om_ref, wo_ref, lg_ref, lb_ref, o_ref = refs
    acc = alpha * x_ref[...]
    acc = acc + jnp.dot(a_ref[...], wo_ref[0:sba_w, :], preferred_element_type=F32)
    acc = acc + jnp.dot(g_ref[...], wo_ref[sba_w:sba_w + gvw, :], preferred_element_type=F32)
    base = sba_w + gvw
    if fuse_mem:
        om = []
        for h in range(mem_heads):
            lanes = slice(h * hd, (h + 1) * hd)
            s = lax.dot_general(mq_ref[:, lanes], mk_ref[:, lanes], (((1,), (1,)), ((), ())),
                                preferred_element_type=F32)
            p = _softmax_rows(s)
            om.append(jnp.dot(p.astype(BF16), mv_ref[:, lanes], preferred_element_type=F32).astype(BF16))
        acc = acc + jnp.dot(jnp.concatenate(om, axis=1), wo_ref[base:, :], preferred_element_type=F32)
    else:
        acc = acc + jnp.dot(om_ref[...], wo_ref[base:, :], preferred_element_type=F32)
    o_ref[...] = _layer_norm(acc, lg_ref[...], lb_ref[...])


def _wo_ln(x2d, o_sba, og, mem_args, wo, ln_g, ln_b, *, tm, alpha, mem_heads, hd, rows_per_batch=None):
    m, d = x2d.shape
    sba_w, gvw = o_sba.shape[1], og.shape[1]
    fuse_mem = rows_per_batch is not None
    row = lambda w: pl.BlockSpec((tm, w), lambda i: (i, 0))
    const = lambda shape: pl.BlockSpec(shape, lambda i: (0, 0), pipeline_mode=pl.Buffered(1))
    if fuse_mem:
        mq, mk, mv = mem_args
        n_mem = mk.shape[0] // (m // rows_per_batch)
        tiles_per_batch = rows_per_batch // tm
        per_batch = pl.BlockSpec((n_mem, mk.shape[1]), lambda i: (i // tiles_per_batch, 0))
        mem_specs = [row(mq.shape[1]), per_batch, per_batch]
    else:
        mem_specs = [row(mem_args[0].shape[1])]
    kern = functools.partial(_wo_ln_kernel, alpha=alpha, sba_w=sba_w, gvw=gvw, mem_heads=mem_heads, hd=hd,
                             fuse_mem=fuse_mem)
    return pl.pallas_call(
        kern,
        grid=(m // tm,),
        in_specs=[row(d), row(sba_w), row(gvw)] + mem_specs + [const(wo.shape), const(ln_g.shape), const(ln_b.shape)],
        out_specs=row(d),
        out_shape=jax.ShapeDtypeStruct((m, d), F32),
        compiler_params=_params(("parallel",)),
        name="wo_ln1",
    )(x2d, o_sba, og, *mem_args, wo, ln_g, ln_b)


def _mlp_init(x_ref, xb_ref, acc_ref):
    xb_ref[...] = x_ref[...].astype(BF16)
    acc_ref[...] = jnp.zeros_like(acc_ref)


def _mlp_accumulate(xb_ref, wu_ref, wd_ref, acc_ref):
    _mlp_down(_mlp_up(xb_ref, wu_ref), wd_ref, acc_ref)


def _mlp_up(xb_ref, wu_ref):
    h = jnp.dot(xb_ref[...], wu_ref[...], preferred_element_type=F32)
    return jnp.square(jnp.maximum(h, 0.0)).astype(BF16)


def _mlp_down(u, wd_ref, acc_ref):
    acc_ref[...] += jnp.dot(u, wd_ref[...], preferred_element_type=F32)


def _mlp_finish(x_ref, acc_ref, lg_ref, lb_ref, o_ref, alpha):
    o_ref[...] = _layer_norm(alpha * x_ref[...] + acc_ref[...], lg_ref[...], lb_ref[...])


def _mlp_kernel(x_ref, wu_ref, wd_ref, lg_ref, lb_ref, o_ref, xb_ref, acc_ref, *, alpha):
    f = pl.program_id(1)

    @pl.when(f == 0)
    def _():
        _mlp_init(x_ref, xb_ref, acc_ref)

    _mlp_accumulate(xb_ref, wu_ref, wd_ref, acc_ref)

    @pl.when(f == pl.num_programs(1) - 1)
    def _():
        _mlp_finish(x_ref, acc_ref, lg_ref, lb_ref, o_ref, alpha)


def _mlp_ln(x1, wu, wd, ln_g, ln_b, *, tm, tf, alpha):
    m, d = x1.shape
    dff = wu.shape[1]
    kern = functools.partial(_mlp_kernel, alpha=alpha)
    return pl.pallas_call(
        kern,
        grid=(m // tm, dff // tf),
        in_specs=[pl.BlockSpec((tm, d), lambda i, f: (i, 0)),
                  pl.BlockSpec((d, tf), lambda i, f: (0, f)),
                  pl.BlockSpec((tf, d), lambda i, f: (f, 0)),
                  pl.BlockSpec((1, d), lambda i, f: (0, 0)),
                  pl.BlockSpec((1, d), lambda i, f: (0, 0))],
        out_specs=pl.BlockSpec((tm, d), lambda i, f: (i, 0)),
        out_shape=jax.ShapeDtypeStruct((m, d), F32),
        scratch_shapes=[pltpu.VMEM((tm, d), BF16), pltpu.VMEM((tm, d), F32)],
        compiler_params=_params(("parallel", "arbitrary")),
        name="mlp_ln2",
    )(x1, wu, wd, ln_g, ln_b)


def _decode_init(acc_ref, run_ref):
    acc_ref[...] = jnp.zeros_like(acc_ref)
    run_ref[...] = jnp.zeros_like(run_ref)


def _dot_2pass(a, b):
    hi = a.astype(BF16)
    lo = (a - hi.astype(F32)).astype(BF16)
    return jnp.dot(hi, b, preferred_element_type=F32) + jnp.dot(lo, b, preferred_element_type=F32)


def _decode_accumulate(bias_ref, q_ref, k_refs, v_refs, u_ref, newer_ref, acc_ref, run_ref, heads):
    z = _decode_scores(bias_ref, q_ref, k_refs, heads)
    _decode_values(z, v_refs, u_ref, newer_ref, acc_ref, run_ref, heads)


def _own_lanes(heads):
    lane_head = lax.broadcasted_iota(jnp.int32, (heads, LANES), 1) & (heads - 1)
    return lane_head == lax.broadcasted_iota(jnp.int32, (heads, LANES), 0)


def _decode_scores(bias_ref, q_ref, k_refs, heads):
    pages_per_step = len(k_refs)
    groups = k_refs[0].shape[0] // LANES
    own = _own_lanes(heads)
    q = q_ref[...].astype(F32)
    qt = lax.dot_general(q, own.astype(F32), (((0,), (0,)), ((), ())), preferred_element_type=F32)
    rows = []
    for j in range(pages_per_step):
        for g in reversed(range(groups)):
            kt = k_refs[j][g * LANES:(g + 1) * LANES, :].T
            rows.append(jnp.sum(kt * qt, axis=0, keepdims=True))
    return (jnp.concatenate(rows, axis=0) + bias_ref[...]) * LN2


def _decode_values(z, v_refs, u_ref, newer_ref, acc_ref, run_ref, heads):
    pages_per_step = len(v_refs)
    groups = v_refs[0].shape[0] // LANES
    own = _own_lanes(heads)
    sp = _softplus(z)
    log_keep = -sp
    log_beta = z - sp
    ext = _dot_2pass(log_keep, u_ref[...])
    incl, total = ext[:, :LANES], ext[:, LANES:]
    run = run_ref[...]
    tot_hi = total.astype(BF16)
    tot_lo = (total - tot_hi.astype(F32)).astype(BF16)
    newer = (jnp.dot(newer_ref[...], tot_hi, preferred_element_type=F32)
             + jnp.dot(newer_ref[...], tot_lo, preferred_element_type=F32))
    wgt = jnp.exp(log_beta + (incl - log_keep) + newer + run)
    run_ref[...] = run + jnp.sum(total, axis=0, keepdims=True)
    acc = acc_ref[...]
    for j in range(pages_per_step):
        first = j * groups
        wj = jnp.concatenate(
            [jnp.where(own, jnp.broadcast_to(wgt[first + groups - 1 - g:first + groups - g], own.shape), 0.0)
             for g in range(groups)], axis=1)
        acc = acc + jnp.dot(wj, v_refs[j][...], preferred_element_type=F32)
    acc_ref[...] = acc


def _sba_decode_kernel(pt_ref, bias_ref, q_ref, *refs, pages_per_step, heads):
    k_refs = refs[:pages_per_step]
    v_refs = refs[pages_per_step:2 * pages_per_step]
    u_ref, newer_ref, o_ref, acc_ref, run_ref = refs[2 * pages_per_step:]
    c = pl.program_id(1)

    @pl.when(c == 0)
    def _():
        _decode_init(acc_ref, run_ref)

    _decode_accumulate(bias_ref, q_ref, k_refs, v_refs, u_ref, newer_ref, acc_ref, run_ref, heads)

    @pl.when(c == pl.num_programs(1) - 1)
    def _():
        o_ref[...] = acc_ref[...]


def _mlp_decode_kernel(pt_ref, x_ref, wu_ref, wd_ref, lg_ref, lb_ref, bias_ref, q_ref, *refs,
                       alpha, pages_per_step, heads):
    k_refs = refs[:pages_per_step]
    v_refs = refs[pages_per_step:2 * pages_per_step]
    u_ref, newer_ref, o_ref, osba_ref, xb_ref, acc_ref, dacc_ref, drun_ref = refs[2 * pages_per_step:]
    f = pl.program_id(1)

    @pl.when(f == 0)
    def _():
        _mlp_init(x_ref, xb_ref, acc_ref)
        _decode_init(dacc_ref, drun_ref)

    u = _mlp_up(xb_ref, wu_ref)
    z = _decode_scores(bias_ref, q_ref, k_refs, heads)
    _mlp_down(u, wd_ref, acc_ref)
    _decode_values(z, v_refs, u_ref, newer_ref, dacc_ref, drun_ref, heads)

    @pl.when(f == pl.num_programs(1) - 1)
    def _():
        _mlp_finish(x_ref, acc_ref, lg_ref, lb_ref, o_ref, alpha)
        osba_ref[...] = dacc_ref[...]


def _decode_operands(q, cache_k, cache_v, layer, page_table, bias, pages_per_step):
    n_seq, n_pages = page_table.shape
    depth, n_phys, page, heads, hd = cache_k.shape
    assert heads & (heads - 1) == 0 and LANES % heads == 0 and (page * heads) % LANES == 0
    rows = page * heads
    ck = cache_k.reshape(depth * n_phys, rows, hd)
    cv = cache_v.reshape(depth * n_phys, rows, hd)
    src = lax.broadcasted_iota(jnp.int32, (LANES, LANES), 0)
    dst = lax.broadcasted_iota(jnp.int32, (LANES, LANES), 1)
    same_head = ((src - dst) & (heads - 1)) == 0
    u = jnp.concatenate([(same_head & (src >= dst)).astype(BF16), same_head.astype(BF16)], axis=1)
    n_blocks = pages_per_step * rows // LANES
    newer = (lax.broadcasted_iota(jnp.int32, (n_blocks, n_blocks), 1)
             < lax.broadcasted_iota(jnp.int32, (n_blocks, n_blocks), 0)).astype(BF16)
    bias_row = jnp.tile(bias.astype(F32), LANES // heads).reshape(1, LANES)

    def page_spec(j):
        def index(b, c, pt):
            logical = n_pages - 1 - (c * pages_per_step + j)
            return (layer * n_phys + pt[b * n_pages + logical], 0, 0)
        return pl.BlockSpec((None, rows, hd), index)

    const = lambda a: pl.BlockSpec(a.shape, lambda b, c, pt: (0, 0))
    seq_spec = pl.BlockSpec((None, heads, hd), lambda b, c, pt: (b, 0, 0))
    head_specs = [const(bias_row), seq_spec]
    page_specs = [page_spec(j) for j in range(pages_per_step)] * 2 + [const(u), const(newer)]
    head_args = (bias_row, q.reshape(n_seq, heads, hd))
    page_args = (*([ck] * pages_per_step), *([cv] * pages_per_step), u, newer)
    scratch = [pltpu.VMEM((heads, hd), F32), pltpu.VMEM((1, LANES), F32)]
    return head_specs, page_specs, head_args, page_args, seq_spec, scratch


def _sba_decode(q, cache_k, cache_v, layer, page_table, bias, *, pages_per_step=16):
    n_seq, n_pages = page_table.shape
    heads, hd = cache_k.shape[3:]
    head_specs, page_specs, head_args, page_args, seq_spec, scratch = _decode_operands(
        q, cache_k, cache_v, layer, page_table, bias, pages_per_step)
    kern = functools.partial(_sba_decode_kernel, pages_per_step=pages_per_step, heads=heads)
    grid_spec = pltpu.PrefetchScalarGridSpec(
        num_scalar_prefetch=1,
        grid=(n_seq, n_pages // pages_per_step),
        in_specs=head_specs + page_specs,
        out_specs=seq_spec,
        scratch_shapes=scratch,
    )
    out = pl.pallas_call(
        kern,
        grid_spec=grid_spec,
        out_shape=jax.ShapeDtypeStruct((n_seq, heads, hd), F32),
        compiler_params=_params(("parallel", "arbitrary")),
        name="sba_decode",
    )(page_table.reshape(-1), *head_args, *page_args)
    return out.reshape(n_seq, heads * hd)


SHARED_GRID_MAX_TM = 512
SHARED_GRID_TF = 512


def _shared_grid_tiles(m, dff, n_seq, n_pages):
    if m % n_seq or dff % SHARED_GRID_TF:
        return None, None
    tm, col_steps = m // n_seq, dff // SHARED_GRID_TF
    if tm % 8 or tm > SHARED_GRID_MAX_TM or n_pages % col_steps:
        return None, None
    return tm, SHARED_GRID_TF


def _mlp_ln_with_decode(x1, wu, wd, ln_g, ln_b, q, cache_k, cache_v, layer, page_table, bias, *, tm, tf, alpha):
    m, d = x1.shape
    dff = wu.shape[1]
    n_seq, n_pages = page_table.shape
    heads, hd = cache_k.shape[3:]
    grid = (m // tm, dff // tf)
    assert grid[0] == n_seq and n_pages % grid[1] == 0
    pages_per_step = n_pages // grid[1]
    head_specs, page_specs, head_args, page_args, seq_spec, scratch = _decode_operands(
        q, cache_k, cache_v, layer, page_table, bias, pages_per_step)
    kern = functools.partial(_mlp_decode_kernel, alpha=alpha, pages_per_step=pages_per_step, heads=heads)
    grid_spec = pltpu.PrefetchScalarGridSpec(
        num_scalar_prefetch=1,
        grid=grid,
        in_specs=[pl.BlockSpec((tm, d), lambda i, f, pt: (i, 0)),
                  pl.BlockSpec((d, tf), lambda i, f, pt: (0, f)),
                  pl.BlockSpec((tf, d), lambda i, f, pt: (f, 0)),
                  pl.BlockSpec((1, d), lambda i, f, pt: (0, 0)),
                  pl.BlockSpec((1, d), lambda i, f, pt: (0, 0))] + head_specs + page_specs,
        out_specs=[pl.BlockSpec((tm, d), lambda i, f, pt: (i, 0)), seq_spec],
        scratch_shapes=[pltpu.VMEM((tm, d), BF16), pltpu.VMEM((tm, d), F32)] + scratch,
    )
    y, o_sba = pl.pallas_call(
        kern,
        grid_spec=grid_spec,
        out_shape=[jax.ShapeDtypeStruct((m, d), F32), jax.ShapeDtypeStruct((n_seq, heads, hd), F32)],
        compiler_params=_params(("parallel", "arbitrary")),
        name="mlp_ln2_sba_decode",
    )(page_table.reshape(-1), x1, wu, wd, ln_g, ln_b, *head_args, *page_args)
    return y, o_sba.reshape(n_seq, heads * hd)


def _gla_step_kernel(q_ref, k_ref, g_ref, v_ref, r_ref, ng_ref, s_ref, o_ref, sn_ref):
    s_new = jnp.exp(g_ref[...]) * s_ref[...].astype(F32) + k_ref[...] * v_ref[...]
    sn_ref[...] = s_new
    o = jnp.sum(q_ref[...] * s_new, axis=2, keepdims=True)
    o_ref[...] = _gla_gate_out(o, r_ref[...], ng_ref[...]).astype(o_ref.dtype)


def _gla_step(q_col, k_col, g_col, v_row, r_row, norm_g, state, layer, *, bb=8):
    depth, n_seq, heads, dk, dv = state.shape
    steps = n_seq // bb
    col = pl.BlockSpec((bb, heads, dk, 1), lambda i: (i, 0, 0, 0))
    rowv = pl.BlockSpec((bb, heads, 1, dv), lambda i: (i, 0, 0, 0))
    st = pl.BlockSpec((bb, heads, dk, dv), lambda i: (i, 0, 0, 0))
    st_in = pl.BlockSpec((bb, heads, dk, dv), lambda i: (layer * steps + i, 0, 0, 0))
    return pl.pallas_call(
        _gla_step_kernel,
        grid=(steps,),
        in_specs=[col, col, col, rowv, rowv, pl.BlockSpec((1, heads, 1, dv), lambda i: (0, 0, 0, 0)), st_in],
        out_specs=[rowv, st],
        out_shape=[jax.ShapeDtypeStruct((n_seq, heads, 1, dv), BF16),
                   jax.ShapeDtypeStruct((n_seq, heads, dk, dv), F32)],
        compiler_params=_params(("parallel",)),
        name="gla_step",
    )(q_col, k_col, g_col, v_row, r_row, norm_g, state.reshape(depth * n_seq, heads, dk, dv))


def _mem_decode_kernel(q_ref, k_ref, v_ref, o_ref, *, heads, seqs):
    rows = k_ref.shape[1]
    own = ((lax.broadcasted_iota(jnp.int32, (heads, rows), 1) & (heads - 1))
           == lax.broadcasted_iota(jnp.int32, (heads, rows), 0))
    for b in range(seqs):
        s = lax.dot_general(q_ref[b].astype(F32), k_ref[b], (((1,), (1,)), ((), ())), preferred_element_type=F32)
        s = jnp.where(own, s, -1e30)
        p = jnp.where(own, jnp.exp(s - jnp.max(s, axis=-1, keepdims=True)), 0.0)
        o = jnp.dot(p, v_ref[b], preferred_element_type=F32)
        o_ref[b] = (o / jnp.sum(p, axis=-1, keepdims=True)).astype(o_ref.dtype)


def _mem_decode(mq, mem_k, mem_v, layer, *, seqs_per_step=8):
    depth, n_seq, n_mem, heads, hd = mem_k.shape
    assert heads & (heads - 1) == 0
    rows = n_mem * heads
    steps = n_seq // seqs_per_step
    kern = functools.partial(_mem_decode_kernel, heads=heads, seqs=seqs_per_step)
    vec = pl.BlockSpec((seqs_per_step, heads, hd), lambda i: (i, 0, 0))
    kv = pl.BlockSpec((seqs_per_step, rows, hd), lambda i: (layer * steps + i, 0, 0))
    out = pl.pallas_call(
        kern,
        grid=(steps,),
        in_specs=[vec, kv, kv],
        out_specs=vec,
        out_shape=jax.ShapeDtypeStruct((n_seq, heads, hd), BF16),
        compiler_params=_params(("parallel",)),
        name="mem_decode",
    )(mq.reshape(n_seq, heads, hd), mem_k.reshape(depth * n_seq, rows, hd), mem_v.reshape(depth * n_seq, rows, hd))
    return out.reshape(n_seq, heads * hd)


def kernel(x_prompt, x_sample, cache_sba_k, cache_sba_v, state_gla, cache_mem_k, cache_mem_v, page_table,
           mem_prompt, w_in, sba_bias, w_gate_up, b_gate, gla_norm_g, w_mem_k, w_mem_v, w_o, ln1_g, ln1_b,
           w_up, w_down, ln2_g, ln2_b):
    depth = w_in.shape[0]
    bp, tp, d = x_prompt.shape
    bs, ts, _ = x_sample.shape
    assert ts == 1, "the decode path handles one new token per sequence"
    _, n_phys, page, sba_h, sba_d = cache_sba_k.shape
    _, _, gla_h, gla_k, gla_v = state_gla.shape
    _, _, n_mem, mem_h, mem_d = cache_mem_k.shape
    rank = w_gate_up.shape[1]
    assert gla_v == LANES and gla_k <= LANES and sba_d == LANES and mem_d == LANES
    sba_w, gkw, gvw, mem_w = sba_h * sba_d, gla_h * gla_k, gla_h * gla_v, mem_h * mem_d
    gp_w = gla_h * LANES
    alpha = (2.0 * depth) ** 0.25
    dims = (sba_w, gkw, gvw, rank, mem_w, gla_h, gla_k)
    proj = functools.partial(_project, sba_w=sba_w, gp_w=gp_w, gvw=gvw, mem_w=mem_w,
                             sba_d=sba_d, gla_k=gla_k, mem_d=mem_d)

    yp = x_prompt.reshape(bp * tp, d)
    ys = x_sample.reshape(bs * ts, d)
    outs = [[] for _ in range(8)]
    for l in range(depth):
        w_packed, wg, bg = _pack_w_in(w_in[l], w_gate_up[l], b_gate[l], dims)
        layer_of = lambda w: w.reshape(w.shape[1:]) if depth == 1 else w[l]
        ng = gla_norm_g[l].reshape(1, gvw)
        l1g, l1b = ln1_g[l].reshape(1, d), ln1_b[l].reshape(1, d)
        l2g, l2b = ln2_g[l].reshape(1, d), ln2_b[l].reshape(1, d)
        bias = sba_bias[l].astype(F32)

        q, kf, vf, kb, vb, gq, gk, gv, gr, gg, mq, wu, wd, wo, wmk, wmv = proj(
            yp, w_packed, wg, bg, tm=256,
            cast_weights=tuple(layer_of(w) for w in (w_up, w_down, w_o, w_mem_k, w_mem_v)))
        o_sba = _sba_prompt(q, kb, vb, bias * LOG2E, batch=bp, seq=tp, heads=sba_h, hd=sba_d)
        og, s_p = _gla_prompt(gq, gk, gv, gg, gr, ng, batch=bp, seq=tp, heads=gla_h, dk=gla_k)
        mkf, mvf, mkb, mvb = _mem_kv(mem_prompt.reshape(bp * n_mem, d), wmk, wmv)
        x1 = _wo_ln(yp, o_sba, og, (mq, mkb, mvb), wo, l1g, l1b, tm=512, alpha=alpha, mem_heads=mem_h, hd=mem_d,
                    rows_per_batch=tp)
        outs[0].append(kf.reshape(bp, tp, sba_h, sba_d))
        outs[1].append(vf.reshape(bp, tp, sba_h, sba_d))
        outs[2].append(s_p)
        outs[3].append(mkf.reshape(bp, n_mem, mem_h, mem_d))
        outs[4].append(mvf.reshape(bp, n_mem, mem_h, mem_d))

        q2, kf2, vf2, _, _, gq2, gk2, gv2, gr2, gg2, mq2 = proj(ys, w_packed, wg, bg, tm=bs)
        decode_args = (q2, cache_sba_k, cache_sba_v, l, page_table, bias * LOG2E)
        tm_shared, tf_shared = _shared_grid_tiles(bp * tp, wu.shape[1], *page_table.shape)
        if tm_shared is not None:
            yp_next, o_sba2 = _mlp_ln_with_decode(x1, wu, wd, l2g, l2b, *decode_args, tm=tm_shared, tf=tf_shared,
                                                  alpha=alpha)
        else:
            yp_next = _mlp_ln(x1, wu, wd, l2g, l2b, tm=512, tf=1024, alpha=alpha)
            o_sba2 = _sba_decode(*decode_args)
        o_sba2 = o_sba2.astype(BF16)
        col = lambda a: a.reshape(bs, gla_h, LANES)[:, :, :gla_k].reshape(bs, gla_h, gla_k, 1)
        rowv = lambda a: a.reshape(bs, gla_h, 1, gla_v)
        og2, s_s = _gla_step(col(gq2), col(gk2), col(gg2), rowv(gv2), rowv(gr2), ng.reshape(1, gla_h, 1, gla_v),
                             state_gla, l)
        o_mem2 = _mem_decode(mq2, cache_mem_k, cache_mem_v, l)
        x1s = _wo_ln(ys, o_sba2, og2.reshape(bs, gvw), (o_mem2,), wo, l1g, l1b, tm=bs, alpha=alpha,
                     mem_heads=mem_h, hd=mem_d)
        ys_next = _mlp_ln(x1s, wu, wd, l2g, l2b, tm=bs, tf=1024, alpha=alpha)
        outs[5].append(kf2.reshape(bs, ts, sba_h, sba_d))
        outs[6].append(vf2.reshape(bs, ts, sba_h, sba_d))
        outs[7].append(s_s)

        yp, ys = yp_next, ys_next

    return (yp.reshape(bp, tp, d), ys.reshape(bs, ts, d), *[jnp.stack(o) for o in outs])
```
